```python
import jax, jax.numpy as jnp
from jax import lax
import numpy as np

D_MODEL = 1024
BATCH = 2
SEQ = 8192
DEPTH = 4

A_WIDTH = D_MODEL // 2
A_GROUPS = 8
A_GROUP_DIM = A_WIDTH // A_GROUPS
CHUNK = 128
B_WIDTH = D_MODEL // 2
CONV_WIDTH = 31
AB_IN = 2 * A_WIDTH + 2 * B_WIDTH
AB_OUT = A_WIDTH + B_WIDTH
HEAD_DIM = 64
C_HEADS = D_MODEL // HEAD_DIM
C_WIDTH = C_HEADS * HEAD_DIM
DILATED_PATTERNS = ((128, 1), (512, 4), (2048, 16))
ROT_DIM = HEAD_DIM // 4
ROPE_THETA = 500000.0
D_FF = 4 * D_MODEL
EPS = 1e-6
NEG = -1e30
N_EVEN = (DEPTH + 1) // 2
N_ODD = DEPTH // 2

kernel_name = "hybrid_gmlp_conv_dilated_attn_encoder"


def rmsnorm(t, g):
    tf = t.astype(jnp.float32)
    y = tf * lax.rsqrt(jnp.mean(tf * tf, axis=-1, keepdims=True) + EPS)
    return (y * g.astype(jnp.float32)).astype(t.dtype)


def layernorm(t, g, b):
    tf = t.astype(jnp.float32)
    mu = jnp.mean(tf, axis=-1, keepdims=True)
    var = jnp.mean(jnp.square(tf - mu), axis=-1, keepdims=True)
    y = (tf - mu) * lax.rsqrt(var + EPS)
    return (y * g.astype(jnp.float32) + b.astype(jnp.float32)).astype(t.dtype)


def rope_tables(seq):
    pos = jnp.arange(seq, dtype=jnp.float32)
    inv_freq = ROPE_THETA ** (-jnp.arange(0, ROT_DIM, 2, dtype=jnp.float32) / ROT_DIM)
    ang = pos[:, None] * inv_freq[None, :]
    return jnp.cos(ang), jnp.sin(ang)


def partial_rope(t, cos, sin):
    half = ROT_DIM // 2
    t1 = t[..., :half].astype(jnp.float32)
    t2 = t[..., half:ROT_DIM].astype(jnp.float32)
    c, s = cos[:, None, :], sin[:, None, :]
    rot = jnp.concatenate([t1 * c - t2 * s, t2 * c + t1 * s], axis=-1).astype(t.dtype)
    return jnp.concatenate([rot, t[..., ROT_DIM:]], axis=-1)


def dilated_band_attention(q, k, v, window, dilation):
    B, S, H, hd = q.shape
    half = window // (2 * dilation)
    blk = half
    L = S // dilation
    nb = -(-L // blk)
    Lp = nb * blk

    def to_strided(t):
        return t.reshape(B, L, dilation, H, hd).transpose(0, 2, 3, 1, 4)

    qs, ks, vs = to_strided(q), to_strided(k), to_strided(v)
    qb = jnp.pad(qs, ((0, 0),) * 3 + ((0, Lp - L), (0, 0))).reshape(B, dilation, H, nb, blk, hd)
    pad_kv = ((0, 0),) * 3 + ((blk, Lp - L + blk), (0, 0))
    ks, vs = jnp.pad(ks, pad_kv), jnp.pad(vs, pad_kv)

    def key_blocks(t):
        return jnp.concatenate(
            [t[..., o * blk:o * blk + Lp, :].reshape(B, dilation, H, nb, blk, hd) for o in range(3)],
            axis=-2)

    kb, vb = key_blocks(ks), key_blocks(vs)
    qi = jnp.arange(nb)[:, None, None] * blk + jnp.arange(blk)[None, :, None]
    kj = jnp.arange(nb)[:, None, None] * blk + jnp.arange(3 * blk)[None, None, :] - blk
    mask = (jnp.abs(kj - qi) <= half) & (kj >= 0) & (kj < L)

    s = jnp.einsum('brhnqd,brhnkd->brhnqk', qb.astype(jnp.float32), kb.astype(jnp.float32)) * (hd ** -0.5)
    s = jnp.where(mask, s, NEG)
    m = jnp.max(s, axis=-1, keepdims=True)
    p = jnp.exp(s - m)
    denom = jnp.sum(p, axis=-1, keepdims=True)
    o = jnp.einsum('brhnqk,brhnkd->brhnqd', p, vb.astype(jnp.float32)) / denom
    lse = (m + jnp.log(denom))[..., 0]
    o = o.reshape(B, dilation, H, Lp, hd)[..., :L, :].transpose(0, 3, 1, 2, 4).reshape(B, S, H, hd)
    lse = lse.reshape(B, dilation, H, Lp)[..., :L].transpose(0, 3, 1, 2).reshape(B, S, H)
    return o, lse


def mixer_ab(h, w_in, sp_w, sp_b, v_g, v_b, conv_w, conv_b, cn_g, cn_b, w_out):
    B, S, _ = h.shape
    z = h @ w_in
    za = jax.nn.gelu(z[..., :2 * A_WIDTH])
    u, v = za[..., :A_WIDTH], za[..., A_WIDTH:]
    v = layernorm(v, v_g, v_b)
    vc = v.reshape(B, S // CHUNK, CHUNK, A_GROUPS, A_GROUP_DIM)
    sv = jnp.einsum('gpq,bnqgc->bnpgc', sp_w, vc) + sp_b.T[:, :, None]
    ya = u * sv.reshape(B, S, A_WIDTH)
    zb = z[..., 2 * A_WIDTH:]
    g = zb[..., :B_WIDTH] * jax.nn.sigmoid(zb[..., B_WIDTH:])
    g = lax.conv_general_dilated(
        g, conv_w[:, None, :].astype(g.dtype), window_strides=(1,),
        padding=((CONV_WIDTH // 2, CONV_WIDTH // 2),),
        dimension_numbers=('NWC', 'WIO', 'NWC'), feature_group_count=B_WIDTH) + conv_b
    yb = jax.nn.silu(layernorm(g, cn_g, cn_b))
    return jnp.concatenate([ya, yb], axis=-1) @ w_out


def mixer_c(h, w_qkv, q_g, k_g, w_out, cos, sin):
    B, S, _ = h.shape
    qkv = (h @ w_qkv).reshape(B, S, 3, C_HEADS, HEAD_DIM)
    q = partial_rope(rmsnorm(qkv[:, :, 0], q_g), cos, sin)
    k = partial_rope(rmsnorm(qkv[:, :, 1], k_g), cos, sin)
    v = qkv[:, :, 2]
    outs, lses = [], []
    for window, dilation in DILATED_PATTERNS:
        o, l = dilated_band_attention(q, k, v, window, dilation)
        outs.append(o)
        lses.append(l)
    wts = jax.nn.softmax(jnp.stack(lses), axis=0)
    o = jnp.einsum('pbsh,pbshd->bshd', wts, jnp.stack(outs))
    return o.reshape(B, S, C_WIDTH).astype(h.dtype) @ w_out


def squared_relu_mlp(h, w1, w2):
    return jnp.square(jax.nn.relu(h @ w1)) @ w2


def setup_inputs(seed: int = 0) -> dict:
    key = jax.random.key(seed)
    ks = jax.random.split(key, 20)

    def nrm(k, shape, scale):
        return jax.random.normal(k, shape, jnp.float32) * scale

    res = (2 * DEPTH) ** -0.5
    return {
        "x": nrm(ks[0], (BATCH, SEQ, D_MODEL), 1.0),
        "mix_norm_g": 1.0 + nrm(ks[1], (DEPTH, D_MODEL), 0.02),
        "mlp_norm_g": 1.0 + nrm(ks[2], (DEPTH, D_MODEL), 0.02),
        "mlp_w1": nrm(ks[3], (DEPTH, D_MODEL, D_FF), D_MODEL ** -0.5),
        "mlp_w2": nrm(ks[4], (DEPTH, D_FF, D_MODEL), D_FF ** -0.5 * res),
        "ab_w_in": nrm(ks[5], (N_EVEN, D_MODEL, AB_IN), D_MODEL ** -0.5),
        "a_spatial_w": nrm(ks[6], (N_EVEN, A_GROUPS, CHUNK, CHUNK), 0.5 * CHUNK ** -0.5),
        "a_spatial_b": 1.0 + nrm(ks[7], (N_EVEN, A_GROUPS, CHUNK), 0.02),
        "a_vnorm_g": 1.0 + nrm(ks[8], (N_EVEN, A_WIDTH), 0.02),
        "a_vnorm_b": nrm(ks[9], (N_EVEN, A_WIDTH), 0.02),
        "b_conv_w": nrm(ks[10], (N_EVEN, CONV_WIDTH, B_WIDTH), CONV_WIDTH ** -0.5),
        "b_conv_b": nrm(ks[11], (N_EVEN, B_WIDTH), 0.02),
        "b_norm_g": 1.0 + nrm(ks[12], (N_EVEN, B_WIDTH), 0.02),
        "b_norm_b": nrm(ks[13], (N_EVEN, B_WIDTH), 0.02),
        "ab_w_out": nrm(ks[14], (N_EVEN, AB_OUT, D_MODEL), AB_OUT ** -0.5 * res),
        "c_w_qkv": nrm(ks[15], (N_ODD, D_MODEL, 3 * C_WIDTH), D_MODEL ** -0.5),
        "c_q_norm_g": 1.0 + nrm(ks[16], (N_ODD, HEAD_DIM), 0.02),
        "c_k_norm_g": 1.0 + nrm(ks[17], (N_ODD, HEAD_DIM), 0.02),
        "c_w_out": nrm(ks[18], (N_ODD, C_WIDTH, D_MODEL), C_WIDTH ** -0.5 * res),
    }


def reference(x, mix_norm_g, mlp_norm_g, mlp_w1, mlp_w2, ab_w_in, a_spatial_w, a_spatial_b,
              a_vnorm_g, a_vnorm_b, b_conv_w, b_conv_b, b_norm_g, b_norm_b, ab_w_out,
              c_w_qkv, c_q_norm_g, c_k_norm_g, c_w_out):
    cos, sin = rope_tables(x.shape[1])
    for layer in range(DEPTH):
        i = layer // 2
        h = rmsnorm(x, mix_norm_g[layer])
        if layer % 2 == 0:
            x = x + mixer_ab(h, ab_w_in[i], a_spatial_w[i], a_spatial_b[i], a_vnorm_g[i], a_vnorm_b[i],
                             b_conv_w[i], b_conv_b[i], b_norm_g[i], b_norm_b[i], ab_w_out[i])
        else:
            x = x + mixer_c(h, c_w_qkv[i], c_q_norm_g[i], c_k_norm_g[i], c_w_out[i], cos, sin)
        h = rmsnorm(x, mlp_norm_g[layer])
        x = x + squared_relu_mlp(h, mlp_w1[layer], mlp_w2[layer])
    return x
```

```python
import functools

import jax
import jax.numpy as jnp
from jax import lax
from jax.experimental import pallas as pl
from jax.experimental.pallas import tpu as pltpu

F32 = jnp.float32
BF16 = jnp.bfloat16

D_MODEL = 1024
D_FF = 4 * D_MODEL
A_WIDTH = D_MODEL // 2
B_WIDTH = D_MODEL // 2
A_GROUPS = 8
CHUNK = 128
CONV_WIDTH = 31
HEAD_DIM = 64
ROT_DIM = HEAD_DIM // 4
ROPE_THETA = 500000.0
DILATIONS = (16, 4, 1)
HALF = 64
EPS = 1e-6
NEG = -1e30

LANES = 128
HALO = 16
TM = 512
FF_CHUNK = 1024
QBLK = 128
KWIN = 2 * QBLK
VMEM_LIMIT = 56 * 1024 * 1024


def _params(sem, vmem=VMEM_LIMIT):
    return pltpu.CompilerParams(dimension_semantics=sem, vmem_limit_bytes=vmem)


def _rms(x, g):
    ms = jnp.mean(x * x, axis=-1, keepdims=True)
    return x * lax.rsqrt(ms + EPS) * g


def _layernorm(x, g, b):
    mu = jnp.mean(x, axis=-1, keepdims=True)
    xc = x - mu
    var = jnp.mean(xc * xc, axis=-1, keepdims=True)
    return xc * lax.rsqrt(var + EPS) * g + b


def _dot(a, b):
    return jnp.dot(a, b, preferred_element_type=F32)


def _mlp_body(x, g, w1_ref, w2_ref):
    h = _rms(x, g).astype(BF16)
    acc = x
    for j in range(D_FF // FF_CHUNK):
        cols = slice(j * FF_CHUNK, (j + 1) * FF_CHUNK)
        a = _dot(h, w1_ref[:, cols])
        a = jnp.square(jnp.maximum(a, 0.0)).astype(BF16)
        acc = acc + _dot(a, w2_ref[cols, :])
    return acc


def _mlp_kernel(x_ref, g_ref, w1_ref, w2_ref, o_ref):
    o_ref[...] = _mlp_body(x_ref[...], g_ref[...], w1_ref, w2_ref)


def _mlp_call(x2d, g, w1, w2):
    n = x2d.shape[0]
    const = lambda i: (0, 0)
    return pl.pallas_call(
        _mlp_kernel,
        out_shape=jax.ShapeDtypeStruct(x2d.shape, F32),
        grid=(n // TM,),
        in_specs=[
            pl.BlockSpec((TM, D_MODEL), lambda i: (i, 0)),
            pl.BlockSpec((1, D_MODEL), const),
            pl.BlockSpec((D_MODEL, D_FF), const),
            pl.BlockSpec((D_FF, D_MODEL), const),
        ],
        out_specs=pl.BlockSpec((TM, D_MODEL), lambda i: (i, 0)),
        compiler_params=_params(("arbitrary",)),
        name="mlp",
    )(x2d, g, w1, w2)


def _proj_kernel(x_ref, a_ref, w_ref, o_ref):
    o_ref[...] = x_ref[...] + _dot(a_ref[...], w_ref[...])


def _proj_call(x2d, a2d, w):
    n = x2d.shape[0]
    return pl.pallas_call(
        _proj_kernel,
        out_shape=jax.ShapeDtypeStruct(x2d.shape, F32),
        grid=(n // TM,),
        in_specs=[
            pl.BlockSpec((TM, D_MODEL), lambda i: (i, 0)),
            pl.BlockSpec((TM, D_MODEL), lambda i: (i, 0)),
            pl.BlockSpec((D_MODEL, D_MODEL), lambda i: (0, 0)),
        ],
        out_specs=pl.BlockSpec((TM, D_MODEL), lambda i: (i, 0)),
        compiler_params=_params(("arbitrary",)),
        name="attn_out_proj",
    )(x2d, a2d, w)


def _mixer_ab_kernel(xp_ref, x_ref, xn_ref, g_ref, win_ref, spw_ref, spb_ref, vg_ref, vb_ref,
                     cw_ref, cb_ref, cng_ref, cnb_ref, wout_ref, o_ref, gs_ref):
    i = pl.program_id(1)
    last = pl.num_programs(1) - 1
    g = g_ref[...]
    x = x_ref[0]
    h = _rms(x, g).astype(BF16)
    h_all = jnp.concatenate(
        [_rms(xp_ref[0], g).astype(BF16), h, _rms(xn_ref[0], g).astype(BF16)], axis=0)

    zb = _dot(h_all, win_ref[:, 2 * A_WIDTH:])
    glu = zb[:, :B_WIDTH] * jax.nn.sigmoid(zb[:, B_WIDTH:])
    row = lax.broadcasted_iota(jnp.int32, glu.shape, 0)
    inside = ((row >= HALO) | (i > 0)) & ((row < TM + HALO) | (i < last))
    gs_ref[...] = jnp.where(inside, glu, 0.0)
    conv = jnp.broadcast_to(cb_ref[...], (TM, B_WIDTH))
    for k in range(CONV_WIDTH):
        conv = conv + cw_ref[k:k + 1, :] * gs_ref[pl.ds(HALO - CONV_WIDTH // 2 + k, TM), :]
    yb = _layernorm(conv, cng_ref[...], cnb_ref[...])
    yb = yb * jax.nn.sigmoid(yb)

    za = jax.nn.gelu(_dot(h, win_ref[:, :2 * A_WIDTH]))
    u = za[:, :A_WIDTH]
    v = _layernorm(za[:, A_WIDTH:], vg_ref[...], vb_ref[...]).astype(BF16)
    lane = lax.broadcasted_iota(jnp.int32, (CHUNK, LANES), 1)
    first_group = lane < LANES // 2
    ya_chunks = []
    for c in range(TM // CHUNK):
        vc = v[c * CHUNK:(c + 1) * CHUNK, :]
        parts = []
        for j in range(A_WIDTH // LANES):
            r = _dot(spw_ref[j], vc[:, j * LANES:(j + 1) * LANES])
            parts.append(jnp.where(first_group, r[:CHUNK], r[CHUNK:]))
        sv = jnp.concatenate(parts, axis=1) + spb_ref[...]
        ya_chunks.append(u[c * CHUNK:(c + 1) * CHUNK, :] * sv)
    ya = jnp.concatenate(ya_chunks, axis=0)

    out = _dot(ya.astype(BF16), wout_ref[:A_WIDTH, :]) + _dot(yb.astype(BF16), wout_ref[A_WIDTH:, :])
    o_ref[0] = x + out


def _mixer_ab_call(x, g, win, spw, spb, vg, vb, cw, cb, cng, cnb, wout):
    b, s, d = x.shape
    nblk = s // TM
    per = TM // HALO
    const2 = lambda bi, i: (0, 0)
    const3 = lambda bi, i: (0, 0, 0)
    return pl.pallas_call(
        _mixer_ab_kernel,
        out_shape=jax.ShapeDtypeStruct(x.shape, F32),
        grid=(b, nblk),
        in_specs=[
            pl.BlockSpec((1, HALO, d), lambda bi, i: (bi, jnp.maximum(i * per - 1, 0), 0)),
            pl.BlockSpec((1, TM, d), lambda bi, i: (bi, i, 0)),
            pl.BlockSpec((1, HALO, d), lambda bi, i: (bi, jnp.minimum((i + 1) * per, s // HALO - 1), 0)),
            pl.BlockSpec((1, d), const2),
            pl.BlockSpec(win.shape, const2),
            pl.BlockSpec(spw.shape, const3),
            pl.BlockSpec(spb.shape, const2),
            pl.BlockSpec((1, A_WIDTH), const2),
            pl.BlockSpec((1, A_WIDTH), const2),
            pl.BlockSpec(cw.shape, const2),
            pl.BlockSpec((1, B_WIDTH), const2),
            pl.BlockSpec((1, B_WIDTH), const2),
            pl.BlockSpec((1, B_WIDTH), const2),
            pl.BlockSpec(wout.shape, const2),
        ],
        out_specs=pl.BlockSpec((1, TM, d), lambda bi, i: (bi, i, 0)),
        scratch_shapes=[pltpu.VMEM((TM + 2 * HALO, B_WIDTH), F32)],
        compiler_params=_params(("arbitrary", "arbitrary")),
        name="mixer_ab",
    )(x, x, x, g, win, spw, spb, vg, vb, cw, cb, cng, cnb, wout)


def _qkv_kernel(x_ref, g_ref, w_ref, qg_ref, kg_ref, cos_ref, sin_ref, q_ref, k_ref, v_ref):
    h = _rms(x_ref[...], g_ref[...]).astype(BF16)
    lane = lax.broadcasted_iota(jnp.int32, (TM, LANES), 1)
    head0 = lane < HEAD_DIM
    low = (lane % HEAD_DIM) < ROT_DIM // 2
    cos = cos_ref[...]
    sin = sin_ref[...]

    def head_norm_rope(t, gain):
        sq = t * t
        s0 = jnp.sum(jnp.where(head0, sq, 0.0), axis=-1, keepdims=True)
        s1 = jnp.sum(jnp.where(head0, 0.0, sq), axis=-1, keepdims=True)
        ms = jnp.where(head0, s0, s1) * (1.0 / HEAD_DIM)
        t = t * lax.rsqrt(ms + EPS) * gain
        partner = jnp.where(low, pltpu.roll(t, LANES - ROT_DIM // 2, 1), pltpu.roll(t, ROT_DIM // 2, 1))
        return t * cos + partner * sin

    for j in range(D_MODEL // LANES):
        cols = slice(j * LANES, (j + 1) * LANES)
        q = _dot(h, w_ref[:, j * LANES:(j + 1) * LANES])
        q_ref[:, cols] = head_norm_rope(q, qg_ref[...]) * (HEAD_DIM ** -0.5)
        k = _dot(h, w_ref[:, D_MODEL + j * LANES:D_MODEL + (j + 1) * LANES])
        k_ref[:, cols] = head_norm_rope(k, kg_ref[...])
    v_ref[...] = _dot(h, w_ref[:, 2 * D_MODEL:])


def _qkv_call(x2d, g, w, qg, kg, cos_t, sin_t, seq):
    n = x2d.shape[0]
    pos_blocks = seq // TM
    const = lambda i: (0, 0)
    row = lambda i: (i, 0)
    out = jax.ShapeDtypeStruct(x2d.shape, F32)
    return pl.pallas_call(
        _qkv_kernel,
        out_shape=(out, out, out),
        grid=(n // TM,),
        in_specs=[
            pl.BlockSpec((TM, D_MODEL), row),
            pl.BlockSpec((1, D_MODEL), const),
            pl.BlockSpec(w.shape, const),
            pl.BlockSpec((1, LANES), const),
            pl.BlockSpec((1, LANES), const),
            pl.BlockSpec((TM, LANES), lambda i: (i % pos_blocks, 0)),
            pl.BlockSpec((TM, LANES), lambda i: (i % pos_blocks, 0)),
        ],
        out_specs=(pl.BlockSpec((TM, D_MODEL), row),) * 3,
        compiler_params=_params(("arbitrary",)),
        name="qkv",
    )(x2d, g, w, qg, kg, cos_t, sin_t)


def _attn_kernel(q_ref, k_ref, v_ref, o_ref, qs, ks, vs, oacc, lacc, bias_ref, *, seq):
    lane = lax.broadcasted_iota(jnp.int32, (QBLK, LANES), 1)
    head0 = lane < HEAD_DIM
    qi = lax.broadcasted_iota(jnp.int32, (QBLK, KWIN), 0)
    kj = lax.broadcasted_iota(jnp.int32, (QBLK, KWIN), 1)
    band = (kj >= qi) & (kj <= qi + 2 * HALF)
    bias_ref[0] = jnp.where(band, 0.0, NEG)
    bias_ref[1] = jnp.where(band & (kj >= HALF), 0.0, NEG)
    bias_ref[2] = jnp.where(band & (kj < KWIN - HALF), 0.0, NEG)
    zeros_pad = jnp.zeros((HALF, LANES), BF16)

    for d in DILATIONS:
        length = seq // d
        seg = length + 2 * HALF
        nblk = length // QBLK

        def regroup(idx, carry, d=d, seg=seg, nblk=nblk):
            r = idx // nblk
            c = idx % nblk
            src = pl.ds(r + c * (QBLK * d), QBLK, stride=d) if d > 1 else pl.ds(pl.multiple_of(c * QBLK, QBLK), QBLK)
            dst = pl.ds(pl.multiple_of(r * seg + HALF + c * QBLK, HALF), QBLK)
            qs[dst, :] = q_ref[0, src, :].astype(BF16)
            ks[dst, :] = k_ref[0, src, :].astype(BF16)
            vs[dst, :] = v_ref[0, src, :].astype(BF16)
            return carry

        def pads(r, carry, seg=seg, length=length):
            lo = pl.ds(pl.multiple_of(r * seg, HALF), HALF)
            hi = pl.ds(pl.multiple_of(r * seg + HALF + length, HALF), HALF)
            for ref in (ks, vs):
                ref[lo, :] = zeros_pad
                ref[hi, :] = zeros_pad
            return carry

        lax.fori_loop(0, d * nblk, regroup, 0)
        lax.fori_loop(0, d, pads, 0)

        def block(idx, carry, d=d, seg=seg, nblk=nblk):
            r = idx // nblk
            n = idx % nblk
            base = pl.multiple_of(r * seg + n * QBLK, HALF)
            qb = qs[pl.ds(base + HALF, QBLK), :]
            kb = ks[pl.ds(base, KWIN), :]
            vb = vs[pl.ds(base, KWIN), :]
            bias = bias_ref[jnp.where(n == 0, 1, jnp.where(n == nblk - 1, 2, 0))]
            zero = jnp.zeros_like(qb)
            qq = jnp.concatenate([jnp.where(head0, qb, zero), jnp.where(head0, zero, qb)], axis=0)
            s = lax.dot_general(qq, kb, (((1,), (1,)), ((), ())), preferred_element_type=F32)
            s = s + jnp.concatenate([bias, bias], axis=0)
            m = jnp.max(s, axis=-1, keepdims=True)
            p = jnp.exp(s - m)
            l = jnp.sum(p, axis=-1, keepdims=True)
            pv = _dot(p.astype(BF16), vb)
            on = pv / l
            lse = m + jnp.log(l)
            o_new = jnp.where(head0, on[:QBLK], on[QBLK:])
            lse_new = jnp.where(head0, lse[:QBLK], lse[QBLK:])
            if d > 1:
                rows = pl.ds(r + n * (QBLK * d), QBLK, stride=d)
            else:
                rows = pl.ds(pl.multiple_of(n * QBLK, QBLK), QBLK)
            if d == DILATIONS[0]:
                oacc[rows, :] = o_new
                lacc[rows, :] = lse_new
            else:
                o_old = oacc[rows, :]
                lse_old = lacc[rows, :]
                mx = jnp.maximum(lse_old, lse_new)
                w_old = jnp.exp(lse_old - mx)
                w_new = jnp.exp(lse_new - mx)
                tot = w_old + w_new
                merged = (o_old * w_old + o_new * w_new) / tot
                if d == DILATIONS[-1]:
                    o_ref[0, rows, :] = merged.astype(o_ref.dtype)
                else:
                    oacc[rows, :] = merged
                    lacc[rows, :] = mx + jnp.log(tot)
            return carry

        lax.fori_loop(0, d * nblk, block, 0)


def _attn_call(q, k, v):
    b, s, d = q.shape
    seg_rows = s + 2 * HALF * max(DILATIONS)
    spec = pl.BlockSpec((1, s, LANES), lambda bi, hp: (bi, 0, hp))
    return pl.pallas_call(
        functools.partial(_attn_kernel, seq=s),
        out_shape=jax.ShapeDtypeStruct((b, s, d), BF16),
        grid=(b, d // LANES),
        in_specs=[spec, spec, spec],
        out_specs=spec,
        scratch_shapes=[
            pltpu.VMEM((seg_rows, LANES), BF16),
            pltpu.VMEM((seg_rows, LANES), BF16),
            pltpu.VMEM((seg_rows, LANES), BF16),
            pltpu.VMEM((s, LANES), F32),
            pltpu.VMEM((s, LANES), F32),
            pltpu.VMEM((3, QBLK, KWIN), F32),
        ],
        compiler_params=_params(("arbitrary", "arbitrary")),
        name="dilated_attn",
    )(q, k, v)


def _rope_tables(seq):
    pos = jnp.arange(seq, dtype=F32)
    inv_freq = ROPE_THETA ** (-jnp.arange(0, ROT_DIM, 2, dtype=F32) / ROT_DIM)
    ang = pos[:, None] * inv_freq[None, :]
    cos, sin = jnp.cos(ang), jnp.sin(ang)
    half = ROT_DIM // 2
    l64 = jnp.arange(LANES) % HEAD_DIM
    f = l64 % half
    cos_t = jnp.where(l64 < ROT_DIM, cos[:, f], 1.0)
    sin_t = jnp.where(l64 < half, -sin[:, f], jnp.where(l64 < ROT_DIM, sin[:, f], 0.0))
    return cos_t, sin_t


def kernel(x, mix_norm_g, mlp_norm_g, mlp_w1, mlp_w2, ab_w_in, a_spatial_w, a_spatial_b, a_vnorm_g, a_vnorm_b, b_conv_w, b_conv_b, b_norm_g, b_norm_b, ab_w_out, c_w_qkv, c_q_norm_g, c_k_norm_g, c_w_out):
    b, s, d = x.shape
    depth = mix_norm_g.shape[0]
    cos_t, sin_t = _rope_tables(s)
    for layer in range(depth):
        i = layer // 2
        g_mix = mix_norm_g[layer][None, :]
        if layer % 2 == 0:
            spw = a_spatial_w[i].astype(BF16).reshape(A_GROUPS // 2, 2 * CHUNK, CHUNK)
            spb = jnp.repeat(a_spatial_b[i].T, A_WIDTH // A_GROUPS, axis=1)
            x = _mixer_ab_call(
                x, g_mix, ab_w_in[i].astype(BF16), spw, spb,
                a_vnorm_g[i][None, :], a_vnorm_b[i][None, :], b_conv_w[i], b_conv_b[i][None, :],
                b_norm_g[i][None, :], b_norm_b[i][None, :], ab_w_out[i].astype(BF16))
        else:
            qg = jnp.tile(c_q_norm_g[i], LANES // HEAD_DIM)[None, :]
            kg = jnp.tile(c_k_norm_g[i], LANES // HEAD_DIM)[None, :]
            q, k, v = _qkv_call(x.reshape(b * s, d), g_mix, c_w_qkv[i].astype(BF16), qg, kg, cos_t, sin_t, s)
            a = _attn_call(q.reshape(b, s, d), k.reshape(b, s, d), v.reshape(b, s, d))
            x = _proj_call(x.reshape(b * s, d), a.reshape(b * s, d), c_w_out[i].astype(BF16)).reshape(b, s, d)
        x = _mlp_call(x.reshape(b * s, d), mlp_norm_g[layer][None, :],
                      mlp_w1[layer].astype(BF16), mlp_w2[layer].astype(BF16)).reshape(b, s, d)
    return x
```

```python
import functools

import jax
import jax.numpy as jnp
from jax import lax
from jax.experimental import pallas as pl
from jax.experimental.pallas import tpu as pltpu

F32 = jnp.float32
BF16 = jnp.bfloat16

D_MODEL = 1024
D_FF = 4 * D_MODEL
A_WIDTH = D_MODEL // 2
B_WIDTH = D_MODEL // 2
A_GROUPS = 8
CHUNK = 128
CONV_WIDTH = 31
HEAD_DIM = 64
ROT_DIM = HEAD_DIM // 4
ROPE_THETA = 500000.0
DILATIONS = (16, 4, 1)
HALF = 64
EPS = 1e-6
NEG = -1e30

LANES = 128
SUBLANES = 8
MXU_COLS = 256
HALO = 16
TM = 512
FF_CHUNK = 1024
QBLK = 128
KWIN = 2 * QBLK
ATTN_UNROLL = 4
VMEM_LIMIT = 56 * 1024 * 1024


def _params(sem, vmem=VMEM_LIMIT):
    return pltpu.CompilerParams(dimension_semantics=sem, vmem_limit_bytes=vmem)


def _rms(x, g):
    ms = jnp.mean(x * x, axis=-1, keepdims=True)
    return x * lax.rsqrt(ms + EPS) * g


def _layernorm(x, g, b):
    mu = jnp.mean(x, axis=-1, keepdims=True)
    xc = x - mu
    var = jnp.mean(xc * xc, axis=-1, keepdims=True)
    return xc * lax.rsqrt(var + EPS) * g + b


def _dot(a, b):
    return jnp.dot(a, b, preferred_element_type=F32)


def _mlp_body(x, g, w1_ref, w2_ref):
    h = _rms(x, g).astype(BF16)
    acc = x
    for j in range(D_FF // FF_CHUNK):
        cols = slice(j * FF_CHUNK, (j + 1) * FF_CHUNK)
        a = _dot(h, w1_ref[:, cols])
        a = jnp.square(jnp.maximum(a, 0.0)).astype(BF16)
        acc = acc + _dot(a, w2_ref[cols, :])
    return acc


def _mlp_kernel(x_ref, g_ref, w1_ref, w2_ref, o_ref):
    o_ref[...] = _mlp_body(x_ref[...], g_ref[...], w1_ref, w2_ref)


def _mlp_call(x2d, g, w1, w2):
    n = x2d.shape[0]
    const = lambda i: (0, 0)
    return pl.pallas_call(
        _mlp_kernel,
        out_shape=jax.ShapeDtypeStruct(x2d.shape, F32),
        grid=(n // TM,),
        in_specs=[
            pl.BlockSpec((TM, D_MODEL), lambda i: (i, 0)),
            pl.BlockSpec((1, D_MODEL), const),
            pl.BlockSpec((D_MODEL, D_FF), const),
            pl.BlockSpec((D_FF, D_MODEL), const),
        ],
        out_specs=pl.BlockSpec((TM, D_MODEL), lambda i: (i, 0)),
        compiler_params=_params(("arbitrary",)),
        name="mlp",
    )(x2d, g, w1, w2)


def _proj_kernel(x_ref, a_ref, w_ref, o_ref):
    o_ref[...] = x_ref[...] + _dot(a_ref[...], w_ref[...])


def _proj_call(x2d, a2d, w):
    n = x2d.shape[0]
    return pl.pallas_call(
        _proj_kernel,
        out_shape=jax.ShapeDtypeStruct(x2d.shape, F32),
        grid=(n // TM,),
        in_specs=[
            pl.BlockSpec((TM, D_MODEL), lambda i: (i, 0)),
            pl.BlockSpec((TM, D_MODEL), lambda i: (i, 0)),
            pl.BlockSpec((D_MODEL, D_MODEL), lambda i: (0, 0)),
        ],
        out_specs=pl.BlockSpec((TM, D_MODEL), lambda i: (i, 0)),
        compiler_params=_params(("arbitrary",)),
        name="attn_out_proj",
    )(x2d, a2d, w)


def _mixer_ab_kernel(xp_ref, x_ref, xn_ref, g_ref, win_ref, spw_ref, spb_ref, vg_ref, vb_ref,
                     cw_ref, cb_ref, cng_ref, cnb_ref, wout_ref, o_ref, gs_ref, sh_ref):
    i = pl.program_id(1)
    last = pl.num_programs(1) - 1
    g = g_ref[...]
    x = x_ref[0]
    h = _rms(x, g).astype(BF16)
    h_all = jnp.concatenate(
        [_rms(xp_ref[0], g).astype(BF16), h, _rms(xn_ref[0], g).astype(BF16)], axis=0)

    zb = _dot(h_all, win_ref[:, 2 * A_WIDTH:])
    glu = zb[:, :B_WIDTH] * jax.nn.sigmoid(zb[:, B_WIDTH:])
    row = lax.broadcasted_iota(jnp.int32, glu.shape, 0)
    inside = ((row >= HALO) | (i > 0)) & ((row < TM + HALO) | (i < last))
    gs_ref[...] = jnp.where(inside, glu, 0.0)
    conv = jnp.broadcast_to(cb_ref[...], (TM, B_WIDTH))
    first_tap = HALO - CONV_WIDTH // 2
    for shift in range(SUBLANES):
        src = gs_ref
        if shift:
            sh_ref[...] = gs_ref[pl.ds(shift, TM + 2 * HALO - SUBLANES), :]
            src = sh_ref
        for base in range(0, 2 * HALO, SUBLANES):
            k = base + shift - first_tap
            if 0 <= k < CONV_WIDTH:
                conv = conv + cw_ref[k:k + 1, :] * src[pl.ds(base, TM), :]
    yb = _layernorm(conv, cng_ref[...], cnb_ref[...])
    yb = yb * jax.nn.sigmoid(yb)

    za = jax.nn.gelu(_dot(h, win_ref[:, :2 * A_WIDTH]))
    u = za[:, :A_WIDTH]
    v = _layernorm(za[:, A_WIDTH:], vg_ref[...], vb_ref[...]).astype(BF16)
    lane = lax.broadcasted_iota(jnp.int32, (CHUNK, LANES), 1)
    first_group = lane < LANES // 2
    ya_chunks = []
    for c in range(TM // CHUNK):
        vc = v[c * CHUNK:(c + 1) * CHUNK, :]
        parts = []
        for j in range(A_WIDTH // LANES):
            r = _dot(spw_ref[j], vc[:, j * LANES:(j + 1) * LANES])
            parts.append(jnp.where(first_group, r[:CHUNK], r[CHUNK:]))
        sv = jnp.concatenate(parts, axis=1) + spb_ref[...]
        ya_chunks.append(u[c * CHUNK:(c + 1) * CHUNK, :] * sv)
    ya = jnp.concatenate(ya_chunks, axis=0)

    out = _dot(ya.astype(BF16), wout_ref[:A_WIDTH, :]) + _dot(yb.astype(BF16), wout_ref[A_WIDTH:, :])
    o_ref[0] = x + out


def _mixer_ab_call(x, g, win, spw, spb, vg, vb, cw, cb, cng, cnb, wout):
    b, s, d = x.shape
    nblk = s // TM
    per = TM // HALO
    const2 = lambda bi, i: (0, 0)
    const3 = lambda bi, i: (0, 0, 0)
    return pl.pallas_call(
        _mixer_ab_kernel,
        out_shape=jax.ShapeDtypeStruct(x.shape, F32),
        grid=(b, nblk),
        in_specs=[
            pl.BlockSpec((1, HALO, d), lambda bi, i: (bi, jnp.maximum(i * per - 1, 0), 0)),
            pl.BlockSpec((1, TM, d), lambda bi, i: (bi, i, 0)),
            pl.BlockSpec((1, HALO, d), lambda bi, i: (bi, jnp.minimum((i + 1) * per, s // HALO - 1), 0)),
            pl.BlockSpec((1, d), const2),
            pl.BlockSpec(win.shape, const2),
            pl.BlockSpec(spw.shape, const3),
            pl.BlockSpec(spb.shape, const2),
            pl.BlockSpec((1, A_WIDTH), const2),
            pl.BlockSpec((1, A_WIDTH), const2),
            pl.BlockSpec(cw.shape, const2),
            pl.BlockSpec((1, B_WIDTH), const2),
            pl.BlockSpec((1, B_WIDTH), const2),
            pl.BlockSpec((1, B_WIDTH), const2),
            pl.BlockSpec(wout.shape, const2),
        ],
        out_specs=pl.BlockSpec((1, TM, d), lambda bi, i: (bi, i, 0)),
        scratch_shapes=[pltpu.VMEM((TM + 2 * HALO, B_WIDTH), F32),
                        pltpu.VMEM((TM + 2 * HALO - SUBLANES, B_WIDTH), F32)],
        compiler_params=_params(("arbitrary", "arbitrary")),
        name="mixer_ab",
    )(x, x, x, g, win, spw, spb, vg, vb, cw, cb, cng, cnb, wout)


def _qkv_kernel(x_ref, g_ref, w_ref, qg_ref, kg_ref, cos_ref, sin_ref, q_ref, k_ref, v_ref):
    h = _rms(x_ref[...], g_ref[...]).astype(BF16)
    lane = lax.broadcasted_iota(jnp.int32, (TM, LANES), 1)
    low = (lane % HEAD_DIM) < ROT_DIM // 2
    cos = cos_ref[...]
    sin = sin_ref[...]
    hr = lax.broadcasted_iota(jnp.int32, (MXU_COLS, MXU_COLS), 0) // HEAD_DIM
    hc = lax.broadcasted_iota(jnp.int32, (MXU_COLS, MXU_COLS), 1) // HEAD_DIM
    same_head = jnp.where(hr == hc, 1.0, 0.0).astype(BF16)

    def head_norm_rope(t4, gain):
        ms = _dot((t4 * t4).astype(BF16), same_head) * (1.0 / HEAD_DIM)
        outs = []
        for half in range(MXU_COLS // LANES):
            sl = slice(half * LANES, (half + 1) * LANES)
            t = t4[:, sl] * lax.rsqrt(ms[:, sl] + EPS) * gain
            partner = jnp.where(low, pltpu.roll(t, LANES - ROT_DIM // 2, 1), pltpu.roll(t, ROT_DIM // 2, 1))
            outs.append(t * cos + partner * sin)
        return jnp.concatenate(outs, axis=1)

    qkv = _dot(h, w_ref[...])
    for j in range(D_MODEL // MXU_COLS):
        cols = slice(j * MXU_COLS, (j + 1) * MXU_COLS)
        q_ref[:, cols] = head_norm_rope(qkv[:, cols], qg_ref[...]) * (HEAD_DIM ** -0.5)
        kcols = slice(D_MODEL + j * MXU_COLS, D_MODEL + (j + 1) * MXU_COLS)
        k_ref[:, cols] = head_norm_rope(qkv[:, kcols], kg_ref[...])
    v_ref[...] = qkv[:, 2 * D_MODEL:]


def _qkv_call(x2d, g, w, qg, kg, cos_t, sin_t, seq):
    n = x2d.shape[0]
    pos_blocks = seq // TM
    const = lambda i: (0, 0)
    row = lambda i: (i, 0)
    out = jax.ShapeDtypeStruct(x2d.shape, F32)
    return pl.pallas_call(
        _qkv_kernel,
        out_shape=(out, out, out),
        grid=(n // TM,),
        in_specs=[
            pl.BlockSpec((TM, D_MODEL), row),
            pl.BlockSpec((1, D_MODEL), const),
            pl.BlockSpec(w.shape, const),
            pl.BlockSpec((1, LANES), const),
            pl.BlockSpec((1, LANES), const),
            pl.BlockSpec((TM, LANES), lambda i: (i % pos_blocks, 0)),
            pl.BlockSpec((TM, LANES), lambda i: (i % pos_blocks, 0)),
        ],
        out_specs=(pl.BlockSpec((TM, D_MODEL), row),) * 3,
        compiler_params=_params(("arbitrary",)),
        name="qkv",
    )(x2d, g, w, qg, kg, cos_t, sin_t)


def _attn_kernel(q_ref, k_ref, v_ref, o_ref, qs, ks, vs, oacc, macc, lacc, bias_ref, *, seq):
    lane = lax.broadcasted_iota(jnp.int32, (QBLK, LANES), 1)
    head0 = lane < HEAD_DIM
    qi = lax.broadcasted_iota(jnp.int32, (QBLK, KWIN), 0)
    kj = lax.broadcasted_iota(jnp.int32, (QBLK, KWIN), 1)
    band = (kj >= qi) & (kj <= qi + 2 * HALF)
    bias_ref[0] = jnp.where(band, 0.0, NEG)
    bias_ref[1] = jnp.where(band & (kj >= HALF), 0.0, NEG)
    bias_ref[2] = jnp.where(band & (kj < KWIN - HALF), 0.0, NEG)
    zeros_pad = jnp.zeros((HALF, LANES), BF16)

    for d in DILATIONS:
        length = seq // d
        seg = length + 2 * HALF
        nblk = length // QBLK

        def regroup(idx, carry, d=d, seg=seg, nblk=nblk):
            r = idx // nblk
            c = idx % nblk
            src = pl.ds(r + c * (QBLK * d), QBLK, stride=d) if d > 1 else pl.ds(pl.multiple_of(c * QBLK, QBLK), QBLK)
            dst = pl.ds(pl.multiple_of(r * seg + HALF + c * QBLK, HALF), QBLK)
            qs[dst, :] = q_ref[0, src, :].astype(BF16)
            ks[dst, :] = k_ref[0, src, :].astype(BF16)
            vs[dst, :] = v_ref[0, src, :].astype(BF16)
            return carry

        def pads(r, carry, seg=seg, length=length):
            lo = pl.ds(pl.multiple_of(r * seg, HALF), HALF)
            hi = pl.ds(pl.multiple_of(r * seg + HALF + length, HALF), HALF)
            for ref in (ks, vs):
                ref[lo, :] = zeros_pad
                ref[hi, :] = zeros_pad
            return carry

        lax.fori_loop(0, d * nblk, regroup, 0)
        lax.fori_loop(0, d, pads, 0)

        def block_compute(idx, d=d, seg=seg, nblk=nblk):
            r = idx // nblk
            n = idx % nblk
            base = pl.multiple_of(r * seg + n * QBLK, HALF)
            qb = qs[pl.ds(base + HALF, QBLK), :]
            kb = ks[pl.ds(base, KWIN), :]
            vb = vs[pl.ds(base, KWIN), :]
            bias = bias_ref[jnp.where(n == 0, 1, jnp.where(n == nblk - 1, 2, 0))]
            zero = jnp.zeros_like(qb)
            qq = jnp.concatenate([jnp.where(head0, qb, zero), jnp.where(head0, zero, qb)], axis=0)
            s = lax.dot_general(qq, kb, (((1,), (1,)), ((), ())), preferred_element_type=F32)
            s = s + jnp.concatenate([bias, bias], axis=0)
            m = jnp.max(s, axis=-1, keepdims=True)
            p = jnp.exp(s - m)
            l = jnp.sum(p, axis=-1, keepdims=True)
            pv = _dot(p.astype(BF16), vb)
            acc_new = jnp.where(head0, pv[:QBLK], pv[QBLK:])
            m_new = jnp.where(head0, m[:QBLK], m[QBLK:])
            l_new = jnp.where(head0, l[:QBLK], l[QBLK:])
            if d > 1:
                rows = pl.ds(r + n * (QBLK * d), QBLK, stride=d)
            else:
                rows = pl.ds(pl.multiple_of(n * QBLK, QBLK), QBLK)
            if d == DILATIONS[0]:
                return rows, acc_new, m_new, l_new
            m_old = macc[rows, :]
            m_tot = jnp.maximum(m_old, m_new)
            w_old = jnp.exp(m_old - m_tot)
            w_new = jnp.exp(m_new - m_tot)
            acc_tot = oacc[rows, :] * w_old + acc_new * w_new
            l_tot = lacc[rows, :] * w_old + l_new * w_new
            if d == DILATIONS[-1]:
                return rows, acc_tot / l_tot, None, None
            return rows, acc_tot, m_tot, l_tot

        def blocks(it, carry, d=d):
            results = [block_compute(it * ATTN_UNROLL + u) for u in range(ATTN_UNROLL)]
            for rows, acc_val, m_val, l_val in results:
                if d == DILATIONS[-1]:
                    o_ref[0, rows, :] = acc_val.astype(o_ref.dtype)
                else:
                    oacc[rows, :] = acc_val
                    macc[rows, :] = m_val
                    lacc[rows, :] = l_val
            return carry

        lax.fori_loop(0, d * nblk // ATTN_UNROLL, blocks, 0)


def _attn_call(q, k, v):
    b, s, d = q.shape
    seg_rows = s + 2 * HALF * max(DILATIONS)
    spec = pl.BlockSpec((1, s, LANES), lambda bi, hp: (bi, 0, hp))
    return pl.pallas_call(
        functools.partial(_attn_kernel, seq=s),
        out_shape=jax.ShapeDtypeStruct((b, s, d), BF16),
        grid=(b, d // LANES),
        in_specs=[spec, spec, spec],
        out_specs=spec,
        scratch_shapes=[
            pltpu.VMEM((seg_rows, LANES), BF16),
            pltpu.VMEM((seg_rows, LANES), BF16),
            pltpu.VMEM((seg_rows, LANES), BF16),
            pltpu.VMEM((s, LANES), F32),
            pltpu.VMEM((s, LANES), F32),
            pltpu.VMEM((s, LANES), F32),
            pltpu.VMEM((3, QBLK, KWIN), F32),
        ],
        compiler_params=_params(("arbitrary", "arbitrary")),
        name="dilated_attn",
    )(q, k, v)


def _rope_tables(seq):
    pos = jnp.arange(seq, dtype=F32)
    inv_freq = ROPE_THETA ** (-jnp.arange(0, ROT_DIM, 2, dtype=F32) / ROT_DIM)
    ang = pos[:, None] * inv_freq[None, :]
    cos, sin = jnp.cos(ang), jnp.sin(ang)
    half = ROT_DIM // 2
    l64 = jnp.arange(LANES) % HEAD_DIM
    f = l64 % half
    cos_t = jnp.where(l64 < ROT_DIM, cos[:, f], 1.0)
    sin_t = jnp.where(l64 < half, -sin[:, f], jnp.where(l64 < ROT_DIM, sin[:, f], 0.0))
    return cos_t, sin_t


def kernel(x, mix_norm_g, mlp_norm_g, mlp_w1, mlp_w2, ab_w_in, a_spatial_w, a_spatial_b, a_vnorm_g, a_vnorm_b, b_conv_w, b_conv_b, b_norm_g, b_norm_b, ab_w_out, c_w_qkv, c_q_norm_g, c_k_norm_g, c_w_out):
    b, s, d = x.shape
    depth = mix_norm_g.shape[0]
    cos_t, sin_t = _rope_tables(s)
    for layer in range(depth):
        i = layer // 2
        g_mix = mix_norm_g[layer][None, :]
        if layer % 2 == 0:
            spw = a_spatial_w[i].astype(BF16).reshape(A_GROUPS // 2, 2 * CHUNK, CHUNK)
            spb = jnp.repeat(a_spatial_b[i].T, A_WIDTH // A_GROUPS, axis=1)
            x = _mixer_ab_call(
                x, g_mix, ab_w_in[i].astype(BF16), spw, spb,
                a_vnorm_g[i][None, :], a_vnorm_b[i][None, :], b_conv_w[i], b_conv_b[i][None, :],
                b_norm_g[i][None, :], b_norm_b[i][None, :], ab_w_out[i].astype(BF16))
        else:
            qg = jnp.tile(c_q_norm_g[i], LANES // HEAD_DIM)[None, :]
            kg = jnp.tile(c_k_norm_g[i], LANES // HEAD_DIM)[None, :]
            q, k, v = _qkv_call(x.reshape(b * s, d), g_mix, c_w_qkv[i].astype(BF16), qg, kg, cos_t, sin_t, s)
            a = _attn_call(q.reshape(b, s, d), k.reshape(b, s, d), v.reshape(b, s, d))
            x = _proj_call(x.reshape(b * s, d), a.reshape(b * s, d), c_w_out[i].astype(BF16)).reshape(b, s, d)
        x = _mlp_call(x.reshape(b * s, d), mlp_norm_g[layer][None, :],
                      mlp_w1[layer].astype(BF16), mlp_w2[layer].astype(BF16)).reshape(b, s, d)
    return x
```

```python
import functools

import jax
import jax.numpy as jnp
from jax import lax
from jax.experimental import pallas as pl
from jax.experimental.pallas import tpu as pltpu

F32 = jnp.float32
BF16 = jnp.bfloat16

D_MODEL = 1024
D_FF = 4 * D_MODEL
A_WIDTH = D_MODEL // 2
B_WIDTH = D_MODEL // 2
A_GROUPS = 8
CHUNK = 128
CONV_WIDTH = 31
HEAD_DIM = 64
ROT_DIM = HEAD_DIM // 4
ROPE_THETA = 500000.0
DILATIONS = (16, 4, 1)
HALF = 64
EPS = 1e-6
NEG = -1e30

LANES = 128
SUBLANES = 8
MXU_COLS = 256
HALO = 16
TM = 512
FF_CHUNK = 1024
QBLK = 128
KWIN = 2 * QBLK
ATTN_GROUP = 2
VMEM_LIMIT = 56 * 1024 * 1024


def _params(sem, vmem=VMEM_LIMIT):
    return pltpu.CompilerParams(dimension_semantics=sem, vmem_limit_bytes=vmem)


def _rms(x, g):
    ms = jnp.mean(x * x, axis=-1, keepdims=True)
    return x * lax.rsqrt(ms + EPS) * g


def _layernorm(x, g, b):
    mu = jnp.mean(x, axis=-1, keepdims=True)
    xc = x - mu
    var = jnp.mean(xc * xc, axis=-1, keepdims=True)
    return xc * lax.rsqrt(var + EPS) * g + b


def _dot(a, b):
    return jnp.dot(a, b, preferred_element_type=F32)


def _mlp_body(x, g, w1_ref, w2_ref):
    h = _rms(x, g).astype(BF16)
    acc = x
    for j in range(D_FF // FF_CHUNK):
        cols = slice(j * FF_CHUNK, (j + 1) * FF_CHUNK)
        a = _dot(h, w1_ref[:, cols])
        a = jnp.square(jnp.maximum(a, 0.0)).astype(BF16)
        acc = acc + _dot(a, w2_ref[cols, :])
    return acc


def _mlp_kernel(x_ref, g_ref, w1_ref, w2_ref, o_ref):
    o_ref[...] = _mlp_body(x_ref[...], g_ref[...], w1_ref, w2_ref)


def _mlp_call(x2d, g, w1, w2):
    n = x2d.shape[0]
    const = lambda i: (0, 0)
    return pl.pallas_call(
        _mlp_kernel,
        out_shape=jax.ShapeDtypeStruct(x2d.shape, F32),
        grid=(n // TM,),
        in_specs=[
            pl.BlockSpec((TM, D_MODEL), lambda i: (i, 0)),
            pl.BlockSpec((1, D_MODEL), const),
            pl.BlockSpec((D_MODEL, D_FF), const),
            pl.BlockSpec((D_FF, D_MODEL), const),
        ],
        out_specs=pl.BlockSpec((TM, D_MODEL), lambda i: (i, 0)),
        compiler_params=_params(("arbitrary",)),
        name="mlp",
    )(x2d, g, w1, w2)


def _proj_mlp_kernel(x_ref, a_ref, wo_ref, g_ref, w1_ref, w2_ref, o_ref):
    x = x_ref[...] + _dot(a_ref[...], wo_ref[...])
    o_ref[...] = _mlp_body(x, g_ref[...], w1_ref, w2_ref)


def _proj_mlp_call(x2d, a2d, wo, g, w1, w2):
    n = x2d.shape[0]
    const = lambda i: (0, 0)
    row = lambda i: (i, 0)
    resident = dict(pipeline_mode=pl.Buffered(1))
    return pl.pallas_call(
        _proj_mlp_kernel,
        out_shape=jax.ShapeDtypeStruct(x2d.shape, F32),
        grid=(n // TM,),
        in_specs=[
            pl.BlockSpec((TM, D_MODEL), row),
            pl.BlockSpec((TM, D_MODEL), row),
            pl.BlockSpec((D_MODEL, D_MODEL), const, **resident),
            pl.BlockSpec((1, D_MODEL), const),
            pl.BlockSpec((D_MODEL, D_FF), const, **resident),
            pl.BlockSpec((D_FF, D_MODEL), const, **resident),
        ],
        out_specs=pl.BlockSpec((TM, D_MODEL), row),
        compiler_params=_params(("arbitrary",)),
        name="attn_out_proj_mlp",
    )(x2d, a2d, wo, g, w1, w2)


def _mixer_ab_kernel(xp_ref, x_ref, xn_ref, g_ref, win_ref, spw_ref, spb_ref, vg_ref, vb_ref,
                     cw_ref, cb_ref, cng_ref, cnb_ref, wout_ref, o_ref, gs_ref, sh_ref):
    i = pl.program_id(1)
    last = pl.num_programs(1) - 1
    g = g_ref[...]
    x = x_ref[0]
    h = _rms(x, g).astype(BF16)
    h_all = jnp.concatenate(
        [_rms(xp_ref[0], g).astype(BF16), h, _rms(xn_ref[0], g).astype(BF16)], axis=0)

    zb = _dot(h_all, win_ref[:, 2 * A_WIDTH:])
    glu = zb[:, :B_WIDTH] * jax.nn.sigmoid(zb[:, B_WIDTH:])
    row = lax.broadcasted_iota(jnp.int32, glu.shape, 0)
    inside = ((row >= HALO) | (i > 0)) & ((row < TM + HALO) | (i < last))
    gs_ref[...] = jnp.where(inside, glu, 0.0)
    conv = jnp.broadcast_to(cb_ref[...], (TM, B_WIDTH))
    first_tap = HALO - CONV_WIDTH // 2
    for shift in range(SUBLANES):
        src = gs_ref
        if shift:
            sh_ref[...] = gs_ref[pl.ds(shift, TM + 2 * HALO - SUBLANES), :]
            src = sh_ref
        for base in range(0, 2 * HALO, SUBLANES):
            k = base + shift - first_tap
            if 0 <= k < CONV_WIDTH:
                conv = conv + cw_ref[k:k + 1, :] * src[pl.ds(base, TM), :]
    yb = _layernorm(conv, cng_ref[...], cnb_ref[...])
    yb = yb * jax.nn.sigmoid(yb)

    za = jax.nn.gelu(_dot(h, win_ref[:, :2 * A_WIDTH]))
    u = za[:, :A_WIDTH]
    v = _layernorm(za[:, A_WIDTH:], vg_ref[...], vb_ref[...]).astype(BF16)
    lane = lax.broadcasted_iota(jnp.int32, (CHUNK, LANES), 1)
    first_group = lane < LANES // 2
    ya_chunks = []
    for c in range(TM // CHUNK):
        vc = v[c * CHUNK:(c + 1) * CHUNK, :]
        parts = []
        for j in range(A_WIDTH // LANES):
            r = _dot(spw_ref[j], vc[:, j * LANES:(j + 1) * LANES])
            parts.append(jnp.where(first_group, r[:CHUNK], r[CHUNK:]))
        sv = jnp.concatenate(parts, axis=1) + spb_ref[...]
        ya_chunks.append(u[c * CHUNK:(c + 1) * CHUNK, :] * sv)
    ya = jnp.concatenate(ya_chunks, axis=0)

    out = _dot(ya.astype(BF16), wout_ref[:A_WIDTH, :]) + _dot(yb.astype(BF16), wout_ref[A_WIDTH:, :])
    o_ref[0] = x + out


def _mixer_ab_call(x, g, win, spw, spb, vg, vb, cw, cb, cng, cnb, wout):
    b, s, d = x.shape
    nblk = s // TM
    per = TM // HALO
    const2 = lambda bi, i: (0, 0)
    const3 = lambda bi, i: (0, 0, 0)
    return pl.pallas_call(
        _mixer_ab_kernel,
        out_shape=jax.ShapeDtypeStruct(x.shape, F32),
        grid=(b, nblk),
        in_specs=[
            pl.BlockSpec((1, HALO, d), lambda bi, i: (bi, jnp.maximum(i * per - 1, 0), 0)),
            pl.BlockSpec((1, TM, d), lambda bi, i: (bi, i, 0)),
            pl.BlockSpec((1, HALO, d), lambda bi, i: (bi, jnp.minimum((i + 1) * per, s // HALO - 1), 0)),
            pl.BlockSpec((1, d), const2),
            pl.BlockSpec(win.shape, const2),
            pl.BlockSpec(spw.shape, const3),
            pl.BlockSpec(spb.shape, const2),
            pl.BlockSpec((1, A_WIDTH), const2),
            pl.BlockSpec((1, A_WIDTH), const2),
            pl.BlockSpec(cw.shape, const2),
            pl.BlockSpec((1, B_WIDTH), const2),
            pl.BlockSpec((1, B_WIDTH), const2),
            pl.BlockSpec((1, B_WIDTH), const2),
            pl.BlockSpec(wout.shape, const2),
        ],
        out_specs=pl.BlockSpec((1, TM, d), lambda bi, i: (bi, i, 0)),
        scratch_shapes=[pltpu.VMEM((TM + 2 * HALO, B_WIDTH), F32),
                        pltpu.VMEM((TM + 2 * HALO - SUBLANES, B_WIDTH), F32)],
        compiler_params=_params(("arbitrary", "arbitrary")),
        name="mixer_ab",
    )(x, x, x, g, win, spw, spb, vg, vb, cw, cb, cng, cnb, wout)


def _qkv_kernel(x_ref, g_ref, w_ref, qg_ref, kg_ref, cos_ref, sin_ref, q_ref, k_ref, v_ref):
    h = _rms(x_ref[...], g_ref[...]).astype(BF16)
    lane = lax.broadcasted_iota(jnp.int32, (TM, LANES), 1)
    low = (lane % HEAD_DIM) < ROT_DIM // 2
    cos = cos_ref[...]
    sin = sin_ref[...]
    hr = lax.broadcasted_iota(jnp.int32, (MXU_COLS, MXU_COLS), 0) // HEAD_DIM
    hc = lax.broadcasted_iota(jnp.int32, (MXU_COLS, MXU_COLS), 1) // HEAD_DIM
    same_head = jnp.where(hr == hc, 1.0, 0.0).astype(BF16)

    def head_norm_rope(t4, gain):
        ms = _dot((t4 * t4).astype(BF16), same_head) * (1.0 / HEAD_DIM)
        outs = []
        for half in range(MXU_COLS // LANES):
            sl = slice(half * LANES, (half + 1) * LANES)
            t = t4[:, sl] * lax.rsqrt(ms[:, sl] + EPS) * gain
            partner = jnp.where(low, pltpu.roll(t, LANES - ROT_DIM // 2, 1), pltpu.roll(t, ROT_DIM // 2, 1))
            outs.append(t * cos + partner * sin)
        return jnp.concatenate(outs, axis=1)

    qkv = _dot(h, w_ref[...])
    for j in range(D_MODEL // MXU_COLS):
        cols = slice(j * MXU_COLS, (j + 1) * MXU_COLS)
        q_ref[:, cols] = head_norm_rope(qkv[:, cols], qg_ref[...]) * (HEAD_DIM ** -0.5)
        kcols = slice(D_MODEL + j * MXU_COLS, D_MODEL + (j + 1) * MXU_COLS)
        k_ref[:, cols] = head_norm_rope(qkv[:, kcols], kg_ref[...])
    v_ref[...] = qkv[:, 2 * D_MODEL:]


def _qkv_call(x2d, g, w, qg, kg, cos_t, sin_t, seq):
    n = x2d.shape[0]
    pos_blocks = seq // TM
    const = lambda i: (0, 0)
    row = lambda i: (i, 0)
    out = jax.ShapeDtypeStruct(x2d.shape, F32)
    return pl.pallas_call(
        _qkv_kernel,
        out_shape=(out, out, out),
        grid=(n // TM,),
        in_specs=[
            pl.BlockSpec((TM, D_MODEL), row),
            pl.BlockSpec((1, D_MODEL), const),
            pl.BlockSpec(w.shape, const),
            pl.BlockSpec((1, LANES), const),
            pl.BlockSpec((1, LANES), const),
            pl.BlockSpec((TM, LANES), lambda i: (i % pos_blocks, 0)),
            pl.BlockSpec((TM, LANES), lambda i: (i % pos_blocks, 0)),
        ],
        out_specs=(pl.BlockSpec((TM, D_MODEL), row),) * 3,
        compiler_params=_params(("arbitrary",)),
        name="qkv",
    )(x2d, g, w, qg, kg, cos_t, sin_t)


def _attn_kernel(q_ref, k_ref, v_ref, o_ref, qs, ks, vs, oacc, macc, lacc, bias_ref,
                 s_scr, p_scr, m_scr, l_scr, *, seq):
    lane = lax.broadcasted_iota(jnp.int32, (QBLK, LANES), 1)
    head0 = lane < HEAD_DIM
    qi = lax.broadcasted_iota(jnp.int32, (QBLK, KWIN), 0)
    kj = lax.broadcasted_iota(jnp.int32, (QBLK, KWIN), 1)
    band = (kj >= qi) & (kj <= qi + 2 * HALF)
    bias_ref[0] = jnp.where(band, 0.0, NEG)
    bias_ref[1] = jnp.where(band & (kj >= HALF), 0.0, NEG)
    bias_ref[2] = jnp.where(band & (kj < KWIN - HALF), 0.0, NEG)
    zeros_pad = jnp.zeros((HALF, LANES), BF16)

    for d in DILATIONS:
        length = seq // d
        seg = length + 2 * HALF
        nblk = length // QBLK

        def regroup(idx, carry, d=d, seg=seg, nblk=nblk):
            r = idx // nblk
            c = idx % nblk
            src = pl.ds(r + c * (QBLK * d), QBLK, stride=d) if d > 1 else pl.ds(pl.multiple_of(c * QBLK, QBLK), QBLK)
            dst = pl.ds(pl.multiple_of(r * seg + HALF + c * QBLK, HALF), QBLK)
            qs[dst, :] = q_ref[0, src, :].astype(BF16)
            ks[dst, :] = k_ref[0, src, :].astype(BF16)
            vs[dst, :] = v_ref[0, src, :].astype(BF16)
            return carry

        def pads(r, carry, seg=seg, length=length):
            lo = pl.ds(pl.multiple_of(r * seg, HALF), HALF)
            hi = pl.ds(pl.multiple_of(r * seg + HALF + length, HALF), HALF)
            for ref in (ks, vs):
                ref[lo, :] = zeros_pad
                ref[hi, :] = zeros_pad
            return carry

        lax.fori_loop(0, d * nblk, regroup, 0)
        lax.fori_loop(0, d, pads, 0)

        def locate(idx, seg=seg, nblk=nblk):
            r = idx // nblk
            n = idx % nblk
            return r, n, pl.multiple_of(r * seg + n * QBLK, HALF)

        def stage_scores(g, slot, nblk=nblk):
            for u in range(ATTN_GROUP):
                _, n, base = locate(g * ATTN_GROUP + u)
                qb = qs[pl.ds(base + HALF, QBLK), :]
                kb = ks[pl.ds(base, KWIN), :]
                bias = bias_ref[jnp.where(n == 0, 1, jnp.where(n == nblk - 1, 2, 0))]
                zero = jnp.zeros_like(qb)
                qq = jnp.concatenate([jnp.where(head0, qb, zero), jnp.where(head0, zero, qb)], axis=0)
                s = lax.dot_general(qq, kb, (((1,), (1,)), ((), ())), preferred_element_type=F32)
                s_scr[slot, u, :QBLK, :] = s[:QBLK] + bias
                s_scr[slot, u, QBLK:, :] = s[QBLK:] + bias

        def stage_softmax(slot):
            for u in range(ATTN_GROUP):
                s = s_scr[slot, u]
                m = jnp.max(s, axis=-1, keepdims=True)
                p = jnp.exp(s - m)
                l = jnp.sum(p, axis=-1, keepdims=True)
                p_scr[slot, u] = p.astype(BF16)
                m_scr[slot, u] = jnp.where(head0, m[:QBLK], m[QBLK:])
                l_scr[slot, u] = jnp.where(head0, l[:QBLK], l[QBLK:])

        def stage_output(g, slot, d=d):
            results = []
            for u in range(ATTN_GROUP):
                r, n, base = locate(g * ATTN_GROUP + u)
                pv = _dot(p_scr[slot, u], vs[pl.ds(base, KWIN), :])
                acc_new = jnp.where(head0, pv[:QBLK], pv[QBLK:])
                m_new = m_scr[slot, u]
                l_new = l_scr[slot, u]
                if d > 1:
                    rows = pl.ds(r + n * (QBLK * d), QBLK, stride=d)
                else:
                    rows = pl.ds(pl.multiple_of(n * QBLK, QBLK), QBLK)
                if d == DILATIONS[0]:
                    results.append((rows, acc_new, m_new, l_new))
                    continue
                m_old = macc[rows, :]
                m_tot = jnp.maximum(m_old, m_new)
                w_old = jnp.exp(m_old - m_tot)
                w_new = jnp.exp(m_new - m_tot)
                acc_tot = oacc[rows, :] * w_old + acc_new * w_new
                l_tot = lacc[rows, :] * w_old + l_new * w_new
                if d == DILATIONS[-1]:
                    results.append((rows, acc_tot / l_tot, None, None))
                else:
                    results.append((rows, acc_tot, m_tot, l_tot))
            for rows, acc_val, m_val, l_val in results:
                if d == DILATIONS[-1]:
                    o_ref[0, rows, :] = acc_val.astype(o_ref.dtype)
                else:
                    oacc[rows, :] = acc_val
                    macc[rows, :] = m_val
                    lacc[rows, :] = l_val

        ngroups = d * nblk // ATTN_GROUP
        stage_scores(0, 0)
        stage_scores(1, 1)
        stage_softmax(0)

        def steady(t, carry):
            g = 2 * t + 2
            stage_scores(g, 0)
            stage_softmax(1)
            stage_output(g - 2, 0)
            stage_scores(g + 1, 1)
            stage_softmax(0)
            stage_output(g - 1, 1)
            return carry

        lax.fori_loop(0, (ngroups - 2) // 2, steady, 0)
        stage_softmax(1)
        stage_output(ngroups - 2, 0)
        stage_output(ngroups - 1, 1)


def _attn_call(q, k, v):
    b, s, d = q.shape
    seg_rows = s + 2 * HALF * max(DILATIONS)
    spec = pl.BlockSpec((1, s, LANES), lambda bi, hp: (bi, 0, hp))
    return pl.pallas_call(
        functools.partial(_attn_kernel, seq=s),
        out_shape=jax.ShapeDtypeStruct((b, s, d), BF16),
        grid=(b, d // LANES),
        in_specs=[spec, spec, spec],
        out_specs=spec,
        scratch_shapes=[
            pltpu.VMEM((seg_rows, LANES), BF16),
            pltpu.VMEM((seg_rows, LANES), BF16),
            pltpu.VMEM((seg_rows, LANES), BF16),
            pltpu.VMEM((s, LANES), F32),
            pltpu.VMEM((s, LANES), F32),
            pltpu.VMEM((s, LANES), F32),
            pltpu.VMEM((3, QBLK, KWIN), F32),
            pltpu.VMEM((2, ATTN_GROUP, 2 * QBLK, KWIN), F32),
            pltpu.VMEM((2, ATTN_GROUP, 2 * QBLK, KWIN), BF16),
            pltpu.VMEM((2, ATTN_GROUP, QBLK, LANES), F32),
            pltpu.VMEM((2, ATTN_GROUP, QBLK, LANES), F32),
        ],
        compiler_params=_params(("arbitrary", "arbitrary")),
        name="dilated_attn",
    )(q, k, v)


def _rope_tables(seq):
    pos = jnp.arange(seq, dtype=F32)
    inv_freq = ROPE_THETA ** (-jnp.arange(0, ROT_DIM, 2, dtype=F32) / ROT_DIM)
    ang = pos[:, None] * inv_freq[None, :]
    cos, sin = jnp.cos(ang), jnp.sin(ang)
    half = ROT_DIM // 2
    l64 = jnp.arange(LANES) % HEAD_DIM
    f = l64 % half
    cos_t = jnp.where(l64 < ROT_DIM, cos[:, f], 1.0)
    sin_t = jnp.where(l64 < half, -sin[:, f], jnp.where(l64 < ROT_DIM, sin[:, f], 0.0))
    return cos_t, sin_t


def kernel(x, mix_norm_g, mlp_norm_g, mlp_w1, mlp_w2, ab_w_in, a_spatial_w, a_spatial_b, a_vnorm_g, a_vnorm_b, b_conv_w, b_conv_b, b_norm_g, b_norm_b, ab_w_out, c_w_qkv, c_q_norm_g, c_k_norm_g, c_w_out):
    b, s, d = x.shape
    depth = mix_norm_g.shape[0]
    cos_t, sin_t = _rope_tables(s)
    for layer in range(depth):
        i = layer // 2
        g_mix = mix_norm_g[layer][None, :]
        if layer % 2 == 0:
            spw = a_spatial_w[i].astype(BF16).reshape(A_GROUPS // 2, 2 * CHUNK, CHUNK)
            spb = jnp.repeat(a_spatial_b[i].T, A_WIDTH // A_GROUPS, axis=1)
            x = _mixer_ab_call(
                x, g_mix, ab_w_in[i].astype(BF16), spw, spb,
                a_vnorm_g[i][None, :], a_vnorm_b[i][None, :], b_conv_w[i], b_conv_b[i][None, :],
                b_norm_g[i][None, :], b_norm_b[i][None, :], ab_w_out[i].astype(BF16))
        else:
            qg = jnp.tile(c_q_norm_g[i], LANES // HEAD_DIM)[None, :]
            kg = jnp.tile(c_k_norm_g[i], LANES // HEAD_DIM)[None, :]
            q, k, v = _qkv_call(x.reshape(b * s, d), g_mix, c_w_qkv[i].astype(BF16), qg, kg, cos_t, sin_t, s)
            a = _attn_call(q.reshape(b, s, d), k.reshape(b, s, d), v.reshape(b, s, d))
            x = _proj_mlp_call(x.reshape(b * s, d), a.reshape(b * s, d), c_w_out[i].astype(BF16),
                               mlp_norm_g[layer][None, :], mlp_w1[layer].astype(BF16),
                               mlp_w2[layer].astype(BF16)).reshape(b, s, d)
            continue
        x = _mlp_call(x.reshape(b * s, d), mlp_norm_g[layer][None, :],
                      mlp_w1[layer].astype(BF16), mlp_w2[layer].astype(BF16)).reshape(b, s, d)
    return x
```

```python
import functools

import jax
import jax.numpy as jnp
from jax import lax
from jax.experimental import pallas as pl
from jax.experimental.pallas import tpu as pltpu

F32 = jnp.float32
BF16 = jnp.bfloat16

D_MODEL = 1024
D_FF = 4 * D_MODEL
A_WIDTH = D_MODEL // 2
B_WIDTH = D_MODEL // 2
A_GROUPS = 8
CHUNK = 128
CONV_WIDTH = 31
HEAD_DIM = 64
ROT_DIM = HEAD_DIM // 4
ROPE_THETA = 500000.0
DILATIONS = (16, 4, 1)
HALF = 64
EPS = 1e-6
NEG = -1e30

LANES = 128
SUBLANES = 8
MXU_COLS = 256
HALO = 16
TM = 512
FF_CHUNK = 512
QBLK = 128
KWIN = 2 * QBLK
ATTN_GROUP = 2
VMEM_LIMIT = 56 * 1024 * 1024


def _params(sem, vmem=VMEM_LIMIT):
    return pltpu.CompilerParams(dimension_semantics=sem, vmem_limit_bytes=vmem)


def _rms(x, g):
    ms = jnp.mean(x * x, axis=-1, keepdims=True)
    return x * lax.rsqrt(ms + EPS) * g


def _layernorm(x, g, b):
    mu = jnp.mean(x, axis=-1, keepdims=True)
    xc = x - mu
    var = jnp.mean(xc * xc, axis=-1, keepdims=True)
    return xc * lax.rsqrt(var + EPS) * g + b


def _dot(a, b):
    return jnp.dot(a, b, preferred_element_type=F32)


def _mlp_steps(x, g, w1_ref, w2_ref):
    h = _rms(x, g).astype(BF16)
    acc = x
    for j in range(D_FF // FF_CHUNK):
        cols = slice(j * FF_CHUNK, (j + 1) * FF_CHUNK)
        a = _dot(h, w1_ref[:, cols])
        a = jnp.square(jnp.maximum(a, 0.0)).astype(BF16)
        acc = acc + _dot(a, w2_ref[cols, :])
        yield acc


def _mlp_body(x, g, w1_ref, w2_ref):
    for acc in _mlp_steps(x, g, w1_ref, w2_ref):
        pass
    return acc


def _proj_mlp_kernel(x_ref, a_ref, wo_ref, g_ref, w1_ref, w2_ref, o_ref):
    x = x_ref[...] + _dot(a_ref[...], wo_ref[...])
    o_ref[...] = _mlp_body(x, g_ref[...], w1_ref, w2_ref)


def _proj_mlp_call(x2d, a2d, wo, g, w1, w2):
    n = x2d.shape[0]
    const = lambda i: (0, 0)
    row = lambda i: (i, 0)
    resident = dict(pipeline_mode=pl.Buffered(1))
    return pl.pallas_call(
        _proj_mlp_kernel,
        out_shape=jax.ShapeDtypeStruct(x2d.shape, F32),
        grid=(n // TM,),
        in_specs=[
            pl.BlockSpec((TM, D_MODEL), row),
            pl.BlockSpec((TM, D_MODEL), row),
            pl.BlockSpec((D_MODEL, D_MODEL), const, **resident),
            pl.BlockSpec((1, D_MODEL), const),
            pl.BlockSpec((D_MODEL, D_FF), const, **resident),
            pl.BlockSpec((D_FF, D_MODEL), const, **resident),
        ],
        out_specs=pl.BlockSpec((TM, D_MODEL), row),
        compiler_params=_params(("arbitrary",)),
        name="attn_out_proj_mlp",
    )(x2d, a2d, wo, g, w1, w2)


def _mixer_mlp_kernel(xp_ref, x_ref, xn_ref, g_ref, win_ref, spw_ref, spb_ref, vg_ref, vb_ref,
                      cw_ref, cb_ref, cng_ref, cnb_ref, wout_ref, g2_ref, w1_ref, w2_ref,
                      o_ref, gs_ref, sh_ref, xm_ref, *, blocks_per_seq):
    t = pl.program_id(0)
    nsteps = pl.num_programs(0)

    def mlp():
        for acc in _mlp_steps(xm_ref[...], g2_ref[...], w1_ref, w2_ref):
            yield
        o_ref[...] = acc
        yield

    def mixer():
        i = t % blocks_per_seq
        last = blocks_per_seq - 1
        g = g_ref[...]
        x = x_ref[...]
        h = _rms(x, g).astype(BF16)
        h_all = jnp.concatenate(
            [_rms(xp_ref[...], g).astype(BF16), h, _rms(xn_ref[...], g).astype(BF16)], axis=0)

        zb = _dot(h_all, win_ref[:, 2 * A_WIDTH:])
        glu = zb[:, :B_WIDTH] * jax.nn.sigmoid(zb[:, B_WIDTH:])
        row = lax.broadcasted_iota(jnp.int32, glu.shape, 0)
        inside = ((row >= HALO) | (i > 0)) & ((row < TM + HALO) | (i < last))
        gs_ref[...] = jnp.where(inside, glu, 0.0)
        yield
        conv = jnp.broadcast_to(cb_ref[...], (TM, B_WIDTH))
        first_tap = HALO - CONV_WIDTH // 2
        for shift in range(SUBLANES):
            src = gs_ref
            if shift:
                sh_ref[...] = gs_ref[pl.ds(shift, TM + 2 * HALO - SUBLANES), :]
                src = sh_ref
            for base in range(0, 2 * HALO, SUBLANES):
                k = base + shift - first_tap
                if 0 <= k < CONV_WIDTH:
                    conv = conv + cw_ref[k:k + 1, :] * src[pl.ds(base, TM), :]
            if shift % 2 == 1 and shift < SUBLANES - 1:
                yield
        yb = _layernorm(conv, cng_ref[...], cnb_ref[...])
        yb = yb * jax.nn.sigmoid(yb)
        yield

        za = jax.nn.gelu(_dot(h, win_ref[:, :2 * A_WIDTH]))
        u = za[:, :A_WIDTH]
        v = _layernorm(za[:, A_WIDTH:], vg_ref[...], vb_ref[...]).astype(BF16)
        yield
        lane = lax.broadcasted_iota(jnp.int32, (CHUNK, LANES), 1)
        first_group = lane < LANES // 2
        ya_chunks = []
        for c in range(TM // CHUNK):
            vc = v[c * CHUNK:(c + 1) * CHUNK, :]
            parts = []
            for j in range(A_WIDTH // LANES):
                r = _dot(spw_ref[j], vc[:, j * LANES:(j + 1) * LANES])
                parts.append(jnp.where(first_group, r[:CHUNK], r[CHUNK:]))
            sv = jnp.concatenate(parts, axis=1) + spb_ref[...]
            ya_chunks.append(u[c * CHUNK:(c + 1) * CHUNK, :] * sv)
            if c % 2 == 1:
                yield
        ya = jnp.concatenate(ya_chunks, axis=0)

        out = _dot(ya.astype(BF16), wout_ref[:A_WIDTH, :]) + _dot(yb.astype(BF16), wout_ref[A_WIDTH:, :])
        xm_ref[...] = x + out
        yield

    def run(*parts):
        live = list(parts)
        while live:
            for part in list(live):
                if next(part, StopIteration) is StopIteration:
                    live.remove(part)

    @pl.when(t == 0)
    def _():
        run(mixer())

    @pl.when((t > 0) & (t < nsteps - 1))
    def _():
        run(mlp(), mixer())

    @pl.when(t == nsteps - 1)
    def _():
        run(mlp())


def _mixer_mlp_call(x2d, seq, g, win, spw, spb, vg, vb, cw, cb, cng, cnb, wout, g2, w1, w2):
    n, d = x2d.shape
    nblk = n // TM
    per = TM // HALO
    const2 = lambda t: (0, 0)
    const3 = lambda t: (0, 0, 0)
    cur = lambda t: jnp.minimum(t, nblk - 1)
    resident = dict(pipeline_mode=pl.Buffered(1))
    return pl.pallas_call(
        functools.partial(_mixer_mlp_kernel, blocks_per_seq=seq // TM),
        out_shape=jax.ShapeDtypeStruct(x2d.shape, F32),
        grid=(nblk + 1,),
        in_specs=[
            pl.BlockSpec((HALO, d), lambda t: (jnp.maximum(cur(t) * per - 1, 0), 0)),
            pl.BlockSpec((TM, d), lambda t: (cur(t), 0)),
            pl.BlockSpec((HALO, d), lambda t: (jnp.minimum((cur(t) + 1) * per, n // HALO - 1), 0)),
            pl.BlockSpec((1, d), const2),
            pl.BlockSpec(win.shape, const2, **resident),
            pl.BlockSpec(spw.shape, const3, **resident),
            pl.BlockSpec(spb.shape, const2, **resident),
            pl.BlockSpec((1, A_WIDTH), const2),
            pl.BlockSpec((1, A_WIDTH), const2),
            pl.BlockSpec(cw.shape, const2),
            pl.BlockSpec((1, B_WIDTH), const2),
            pl.BlockSpec((1, B_WIDTH), const2),
            pl.BlockSpec((1, B_WIDTH), const2),
            pl.BlockSpec(wout.shape, const2, **resident),
            pl.BlockSpec((1, d), const2),
            pl.BlockSpec((D_MODEL, D_FF), const2, **resident),
            pl.BlockSpec((D_FF, D_MODEL), const2, **resident),
        ],
        out_specs=pl.BlockSpec((TM, d), lambda t: (jnp.maximum(t - 1, 0), 0)),
        scratch_shapes=[pltpu.VMEM((TM + 2 * HALO, B_WIDTH), F32),
                        pltpu.VMEM((TM + 2 * HALO - SUBLANES, B_WIDTH), F32),
                        pltpu.VMEM((TM, D_MODEL), F32)],
        compiler_params=_params(("arbitrary",)),
        name="mixer_ab_mlp",
    )(x2d, x2d, x2d, g, win, spw, spb, vg, vb, cw, cb, cng, cnb, wout, g2, w1, w2)


def _qkv_kernel(x_ref, g_ref, w_ref, qg_ref, kg_ref, cos_ref, sin_ref, q_ref, k_ref, v_ref, raw_ref):
    i = pl.program_id(0)
    nsteps = pl.num_programs(0)

    def project():
        h = _rms(x_ref[...], g_ref[...]).astype(BF16)
        raw_ref[...] = _dot(h, w_ref[...])

    def finish():
        lane = lax.broadcasted_iota(jnp.int32, (TM, LANES), 1)
        low = (lane % HEAD_DIM) < ROT_DIM // 2
        cos = cos_ref[...]
        sin = sin_ref[...]
        hr = lax.broadcasted_iota(jnp.int32, (MXU_COLS, MXU_COLS), 0) // HEAD_DIM
        hc = lax.broadcasted_iota(jnp.int32, (MXU_COLS, MXU_COLS), 1) // HEAD_DIM
        same_head = jnp.where(hr == hc, 1.0, 0.0).astype(BF16)

        def head_norm_rope(t4, gain):
            ms = _dot((t4 * t4).astype(BF16), same_head) * (1.0 / HEAD_DIM)
            outs = []
            for half in range(MXU_COLS // LANES):
                sl = slice(half * LANES, (half + 1) * LANES)
                t = t4[:, sl] * lax.rsqrt(ms[:, sl] + EPS) * gain
                partner = jnp.where(low, pltpu.roll(t, LANES - ROT_DIM // 2, 1), pltpu.roll(t, ROT_DIM // 2, 1))
                outs.append(t * cos + partner * sin)
            return jnp.concatenate(outs, axis=1)

        for j in range(D_MODEL // MXU_COLS):
            cols = slice(j * MXU_COLS, (j + 1) * MXU_COLS)
            q_ref[:, cols] = head_norm_rope(raw_ref[:, cols], qg_ref[...]) * (HEAD_DIM ** -0.5)
            kcols = slice(D_MODEL + j * MXU_COLS, D_MODEL + (j + 1) * MXU_COLS)
            k_ref[:, cols] = head_norm_rope(raw_ref[:, kcols], kg_ref[...])
        v_ref[...] = raw_ref[:, 2 * D_MODEL:]

    @pl.when(i == 0)
    def _():
        project()

    @pl.when((i > 0) & (i < nsteps - 1))
    def _():
        finish()
        project()

    @pl.when(i == nsteps - 1)
    def _():
        finish()


def _qkv_call(x2d, g, w, qg, kg, cos_t, sin_t, seq):
    n = x2d.shape[0]
    nblk = n // TM
    pos_blocks = seq // TM
    const = lambda i: (0, 0)
    lagged = lambda i: (jnp.maximum(i - 1, 0), 0)
    lagged_pos = lambda i: (jnp.maximum(i - 1, 0) % pos_blocks, 0)
    out = jax.ShapeDtypeStruct(x2d.shape, F32)
    return pl.pallas_call(
        _qkv_kernel,
        out_shape=(out, out, out),
        grid=(nblk + 1,),
        in_specs=[
            pl.BlockSpec((TM, D_MODEL), lambda i: (jnp.minimum(i, nblk - 1), 0)),
            pl.BlockSpec((1, D_MODEL), const),
            pl.BlockSpec(w.shape, const, pipeline_mode=pl.Buffered(1)),
            pl.BlockSpec((1, LANES), const),
            pl.BlockSpec((1, LANES), const),
            pl.BlockSpec((TM, LANES), lagged_pos),
            pl.BlockSpec((TM, LANES), lagged_pos),
        ],
        out_specs=(pl.BlockSpec((TM, D_MODEL), lagged),) * 3,
        scratch_shapes=[pltpu.VMEM((TM, 3 * D_MODEL), F32)],
        compiler_params=_params(("arbitrary",)),
        name="qkv",
    )(x2d, g, w, qg, kg, cos_t, sin_t)


def _attn_kernel(q_ref, k_ref, v_ref, o_ref, qs, ks, vs, oacc, macc, lacc, bias_ref,
                 s_scr, p_scr, m_scr, l_scr, *, seq):
    lane = lax.broadcasted_iota(jnp.int32, (QBLK, LANES), 1)
    head0 = lane < HEAD_DIM
    qi = lax.broadcasted_iota(jnp.int32, (QBLK, KWIN), 0)
    kj = lax.broadcasted_iota(jnp.int32, (QBLK, KWIN), 1)
    band = (kj >= qi) & (kj <= qi + 2 * HALF)
    bias_ref[0] = jnp.where(band, 0.0, NEG)
    bias_ref[1] = jnp.where(band & (kj >= HALF), 0.0, NEG)
    bias_ref[2] = jnp.where(band & (kj < KWIN - HALF), 0.0, NEG)
    zeros_pad = jnp.zeros((HALF, LANES), BF16)

    for d in DILATIONS:
        length = seq // d
        seg = length + 2 * HALF
        nblk = length // QBLK

        def regroup(idx, carry, d=d, seg=seg, nblk=nblk):
            r = idx // nblk
            c = idx % nblk
            src = pl.ds(r + c * (QBLK * d), QBLK, stride=d) if d > 1 else pl.ds(pl.multiple_of(c * QBLK, QBLK), QBLK)
            dst = pl.ds(pl.multiple_of(r * seg + HALF + c * QBLK, HALF), QBLK)
            qs[dst, :] = q_ref[0, src, :].astype(BF16)
            ks[dst, :] = k_ref[0, src, :].astype(BF16)
            vs[dst, :] = v_ref[0, src, :].astype(BF16)
            return carry

        def pads(r, carry, seg=seg, length=length):
            lo = pl.ds(pl.multiple_of(r * seg, HALF), HALF)
            hi = pl.ds(pl.multiple_of(r * seg + HALF + length, HALF), HALF)
            for ref in (ks, vs):
                ref[lo, :] = zeros_pad
                ref[hi, :] = zeros_pad
            return carry

        lax.fori_loop(0, d * nblk, regroup, 0)
        lax.fori_loop(0, d, pads, 0)

        def locate(idx, seg=seg, nblk=nblk):
            r = idx // nblk
            n = idx % nblk
            return r, n, pl.multiple_of(r * seg + n * QBLK, HALF)

        def stage_scores(g, slot, nblk=nblk):
            for u in range(ATTN_GROUP):
                _, n, base = locate(g * ATTN_GROUP + u)
                qb = qs[pl.ds(base + HALF, QBLK), :]
                kb = ks[pl.ds(base, KWIN), :]
                bias = bias_ref[jnp.where(n == 0, 1, jnp.where(n == nblk - 1, 2, 0))]
                zero = jnp.zeros_like(qb)
                qq = jnp.concatenate([jnp.where(head0, qb, zero), jnp.where(head0, zero, qb)], axis=0)
                s = lax.dot_general(qq, kb, (((1,), (1,)), ((), ())), preferred_element_type=F32)
                s_scr[slot, u, :QBLK, :] = s[:QBLK] + bias
                s_scr[slot, u, QBLK:, :] = s[QBLK:] + bias

        def stage_softmax(slot):
            for u in range(ATTN_GROUP):
                s = s_scr[slot, u]
                m = jnp.max(s, axis=-1, keepdims=True)
                p = jnp.exp(s - m)
                l = jnp.sum(p, axis=-1, keepdims=True)
                p_scr[slot, u] = p.astype(BF16)
                m_scr[slot, u] = jnp.where(head0, m[:QBLK], m[QBLK:])
                l_scr[slot, u] = jnp.where(head0, l[:QBLK], l[QBLK:])

        def stage_output(g, slot, d=d):
            results = []
            for u in range(ATTN_GROUP):
                r, n, base = locate(g * ATTN_GROUP + u)
                pv = _dot(p_scr[slot, u], vs[pl.ds(base, KWIN), :])
                acc_new = jnp.where(head0, pv[:QBLK], pv[QBLK:])
                m_new = m_scr[slot, u]
                l_new = l_scr[slot, u]
                if d > 1:
                    rows = pl.ds(r + n * (QBLK * d), QBLK, stride=d)
                else:
                    rows = pl.ds(pl.multiple_of(n * QBLK, QBLK), QBLK)
                if d == DILATIONS[0]:
                    results.append((rows, acc_new, m_new, l_new))
                    continue
                m_old = macc[rows, :]
                m_tot = jnp.maximum(m_old, m_new)
                w_old = jnp.exp(m_old - m_tot)
                w_new = jnp.exp(m_new - m_tot)
                acc_tot = oacc[rows, :] * w_old + acc_new * w_new
                l_tot = lacc[rows, :] * w_old + l_new * w_new
                if d == DILATIONS[-1]:
                    results.append((rows, acc_tot / l_tot, None, None))
                else:
                    results.append((rows, acc_tot, m_tot, l_tot))
            for rows, acc_val, m_val, l_val in results:
                if d == DILATIONS[-1]:
                    o_ref[0, rows, :] = acc_val.astype(o_ref.dtype)
                else:
                    oacc[rows, :] = acc_val
                    macc[rows, :] = m_val
                    lacc[rows, :] = l_val

        ngroups = d * nblk // ATTN_GROUP
        stage_scores(0, 0)
        stage_scores(1, 1)
        stage_softmax(0)

        def steady(t, carry):
            g = 2 * t + 2
            stage_scores(g, 0)
            stage_softmax(1)
            stage_output(g - 2, 0)
            stage_scores(g + 1, 1)
            stage_softmax(0)
            stage_output(g - 1, 1)
            return carry

        lax.fori_loop(0, (ngroups - 2) // 2, steady, 0)
        stage_softmax(1)
        stage_output(ngroups - 2, 0)
        stage_output(ngroups - 1, 1)


def _attn_call(q, k, v):
    b, s, d = q.shape
    seg_rows = s + 2 * HALF * max(DILATIONS)
    spec = pl.BlockSpec((1, s, LANES), lambda bi, hp: (bi, 0, hp))
    return pl.pallas_call(
        functools.partial(_attn_kernel, seq=s),
        out_shape=jax.ShapeDtypeStruct((b, s, d), BF16),
        grid=(b, d // LANES),
        in_specs=[spec, spec, spec],
        out_specs=spec,
        scratch_shapes=[
            pltpu.VMEM((seg_rows, LANES), BF16),
            pltpu.VMEM((seg_rows, LANES), BF16),
            pltpu.VMEM((seg_rows, LANES), BF16),
            pltpu.VMEM((s, LANES), F32),
            pltpu.VMEM((s, LANES), F32),
            pltpu.VMEM((s, LANES), F32),
            pltpu.VMEM((3, QBLK, KWIN), F32),
            pltpu.VMEM((2, ATTN_GROUP, 2 * QBLK, KWIN), F32),
            pltpu.VMEM((2, ATTN_GROUP, 2 * QBLK, KWIN), BF16),
            pltpu.VMEM((2, ATTN_GROUP, QBLK, LANES), F32),
            pltpu.VMEM((2, ATTN_GROUP, QBLK, LANES), F32),
        ],
        compiler_params=_params(("arbitrary", "arbitrary")),
        name="dilated_attn",
    )(q, k, v)


def _rope_tables(seq):
    pos = jnp.arange(seq, dtype=F32)
    inv_freq = ROPE_THETA ** (-jnp.arange(0, ROT_DIM, 2, dtype=F32) / ROT_DIM)
    ang = pos[:, None] * inv_freq[None, :]
    cos, sin = jnp.cos(ang), jnp.sin(ang)
    half = ROT_DIM // 2
    l64 = jnp.arange(LANES) % HEAD_DIM
    f = l64 % half
    cos_t = jnp.where(l64 < ROT_DIM, cos[:, f], 1.0)
    sin_t = jnp.where(l64 < half, -sin[:, f], jnp.where(l64 < ROT_DIM, sin[:, f], 0.0))
    return cos_t, sin_t


def kernel(x, mix_norm_g, mlp_norm_g, mlp_w1, mlp_w2, ab_w_in, a_spatial_w, a_spatial_b, a_vnorm_g, a_vnorm_b, b_conv_w, b_conv_b, b_norm_g, b_norm_b, ab_w_out, c_w_qkv, c_q_norm_g, c_k_norm_g, c_w_out):
    b, s, d = x.shape
    depth = mix_norm_g.shape[0]
    cos_t, sin_t = _rope_tables(s)
    for layer in range(depth):
        i = layer // 2
        g_mix = mix_norm_g[layer][None, :]
        if layer % 2 == 0:
            spw = a_spatial_w[i].astype(BF16).reshape(A_GROUPS // 2, 2 * CHUNK, CHUNK)
            spb = jnp.repeat(a_spatial_b[i].T, A_WIDTH // A_GROUPS, axis=1)
            x = _mixer_mlp_call(
                x.reshape(b * s, d), s, g_mix, ab_w_in[i].astype(BF16), spw, spb,
                a_vnorm_g[i][None, :], a_vnorm_b[i][None, :], b_conv_w[i], b_conv_b[i][None, :],
                b_norm_g[i][None, :], b_norm_b[i][None, :], ab_w_out[i].astype(BF16),
                mlp_norm_g[layer][None, :], mlp_w1[layer].astype(BF16),
                mlp_w2[layer].astype(BF16)).reshape(b, s, d)
        else:
            qg = jnp.tile(c_q_norm_g[i], LANES // HEAD_DIM)[None, :]
            kg = jnp.tile(c_k_norm_g[i], LANES // HEAD_DIM)[None, :]
            q, k, v = _qkv_call(x.reshape(b * s, d), g_mix, c_w_qkv[i].astype(BF16), qg, kg, cos_t, sin_t, s)
            a = _attn_call(q.reshape(b, s, d), k.reshape(b, s, d), v.reshape(b, s, d))
            x = _proj_mlp_call(x.reshape(b * s, d), a.reshape(b * s, d), c_w_out[i].astype(BF16),
                               mlp_norm_g[layer][None, :], mlp_w1[layer].astype(BF16),
                               mlp_w2[layer].astype(BF16)).reshape(b, s, d)
    return x
```

```python
import functools

import jax
import jax.numpy as jnp
from jax import lax
from jax.experimental import pallas as pl
from jax.experimental.pallas import tpu as pltpu

F32 = jnp.float32
BF16 = jnp.bfloat16

D_MODEL = 1024
D_FF = 4 * D_MODEL
A_WIDTH = D_MODEL // 2
B_WIDTH = D_MODEL // 2
A_GROUPS = 8
CHUNK = 128
CONV_WIDTH = 31
HEAD_DIM = 64
ROT_DIM = HEAD_DIM // 4
ROPE_THETA = 500000.0
DILATIONS = (16, 4, 1)
HALF = 64
EPS = 1e-6
NEG = -1e30

LANES = 128
SUBLANES = 8
MXU_COLS = 256
HALO = 16
TM = 512
FF_CHUNK = 512
QBLK = 128
KWIN = 2 * QBLK
ATTN_GROUP = 2
REGROUP_HOP = 4
VMEM_LIMIT = 56 * 1024 * 1024


def _params(sem, vmem=VMEM_LIMIT):
    return pltpu.CompilerParams(dimension_semantics=sem, vmem_limit_bytes=vmem)


def _rms(x, g):
    ms = jnp.mean(x * x, axis=-1, keepdims=True)
    return x * lax.rsqrt(ms + EPS) * g


def _layernorm(x, g, b):
    mu = jnp.mean(x, axis=-1, keepdims=True)
    xc = x - mu
    var = jnp.mean(xc * xc, axis=-1, keepdims=True)
    return xc * lax.rsqrt(var + EPS) * g + b


def _dot(a, b):
    return jnp.dot(a, b, preferred_element_type=F32)


def _layer_block(stack, index):
    zeros = (0,) * (stack.ndim - 1)
    return pl.BlockSpec((None,) + stack.shape[1:], lambda *_: (index,) + zeros, pipeline_mode=pl.Buffered(1))


def _mlp_steps(x, g, w1_ref, w2_ref):
    h = _rms(x, g).astype(BF16)
    acc = x
    for j in range(D_FF // FF_CHUNK):
        cols = slice(j * FF_CHUNK, (j + 1) * FF_CHUNK)
        a = _dot(h, w1_ref[:, cols])
        a = jnp.square(jnp.maximum(a, 0.0)).astype(BF16)
        acc = acc + _dot(a, w2_ref[cols, :])
        yield acc


def _mlp_body(x, g, w1_ref, w2_ref):
    for acc in _mlp_steps(x, g, w1_ref, w2_ref):
        pass
    return acc


def _proj_mlp_kernel(x_ref, a_ref, wo_ref, g_ref, w1_ref, w2_ref, o_ref):
    x = x_ref[...] + _dot(a_ref[...], wo_ref[...])
    o_ref[...] = _mlp_body(x, g_ref[...], w1_ref, w2_ref)


def _proj_mlp_call(x2d, a2d, wo_stack, odd_index, g, w1_stack, w2_stack, layer):
    n = x2d.shape[0]
    const = lambda i: (0, 0)
    row = lambda i: (i, 0)
    return pl.pallas_call(
        _proj_mlp_kernel,
        out_shape=jax.ShapeDtypeStruct(x2d.shape, F32),
        grid=(n // TM,),
        in_specs=[
            pl.BlockSpec((TM, D_MODEL), row),
            pl.BlockSpec((TM, D_MODEL), row),
            _layer_block(wo_stack, odd_index),
            pl.BlockSpec((1, D_MODEL), const),
            _layer_block(w1_stack, layer),
            _layer_block(w2_stack, layer),
        ],
        out_specs=pl.BlockSpec((TM, D_MODEL), row),
        compiler_params=_params(("arbitrary",)),
        name="attn_out_proj_mlp",
    )(x2d, a2d, wo_stack, g, w1_stack, w2_stack)


def _mixer_mlp_kernel(xp_ref, x_ref, xn_ref, g_ref, win_ref, spw_ref, spb_ref, vg_ref, vb_ref,
                      cw_ref, cb_ref, cng_ref, cnb_ref, wout_ref, g2_ref, w1_ref, w2_ref,
                      o_ref, gs_ref, sh_ref, xm_ref, *, blocks_per_seq):
    t = pl.program_id(0)
    nsteps = pl.num_programs(0)

    def mlp():
        for acc in _mlp_steps(xm_ref[...], g2_ref[...], w1_ref, w2_ref):
            yield
        o_ref[...] = acc
        yield

    def mixer():
        i = t % blocks_per_seq
        last = blocks_per_seq - 1
        g = g_ref[...]
        x = x_ref[...]
        h = _rms(x, g).astype(BF16)
        h_all = jnp.concatenate(
            [_rms(xp_ref[...], g).astype(BF16), h, _rms(xn_ref[...], g).astype(BF16)], axis=0)

        zb = _dot(h_all, win_ref[:, 2 * A_WIDTH:])
        glu = zb[:, :B_WIDTH] * jax.nn.sigmoid(zb[:, B_WIDTH:])
        row = lax.broadcasted_iota(jnp.int32, glu.shape, 0)
        inside = ((row >= HALO) | (i > 0)) & ((row < TM + HALO) | (i < last))
        gs_ref[...] = jnp.where(inside, glu, 0.0)
        yield
        conv = jnp.broadcast_to(cb_ref[...], (TM, B_WIDTH))
        first_tap = HALO - CONV_WIDTH // 2
        for shift in range(SUBLANES):
            src = gs_ref
            if shift:
                sh_ref[...] = gs_ref[pl.ds(shift, TM + 2 * HALO - SUBLANES), :]
                src = sh_ref
            for base in range(0, 2 * HALO, SUBLANES):
                k = base + shift - first_tap
                if 0 <= k < CONV_WIDTH:
                    conv = conv + cw_ref[k:k + 1, :] * src[pl.ds(base, TM), :]
            if shift % 2 == 1 and shift < SUBLANES - 1:
                yield
        yb = _layernorm(conv, cng_ref[...], cnb_ref[...])
        yb = yb * jax.nn.sigmoid(yb)
        yield

        za = jax.nn.gelu(_dot(h, win_ref[:, :2 * A_WIDTH]))
        u = za[:, :A_WIDTH]
        v = _layernorm(za[:, A_WIDTH:], vg_ref[...], vb_ref[...]).astype(BF16)
        yield
        lane = lax.broadcasted_iota(jnp.int32, (CHUNK, LANES), 1)
        first_group = lane < LANES // 2
        ya_chunks = []
        for c in range(TM // CHUNK):
            vc = v[c * CHUNK:(c + 1) * CHUNK, :]
            parts = []
            for j in range(A_WIDTH // LANES):
                r = _dot(spw_ref[j], vc[:, j * LANES:(j + 1) * LANES])
                parts.append(jnp.where(first_group, r[:CHUNK], r[CHUNK:]))
            sv = jnp.concatenate(parts, axis=1) + spb_ref[...]
            ya_chunks.append(u[c * CHUNK:(c + 1) * CHUNK, :] * sv)
            if c % 2 == 1:
                yield
        ya = jnp.concatenate(ya_chunks, axis=0)

        out = _dot(ya.astype(BF16), wout_ref[:A_WIDTH, :]) + _dot(yb.astype(BF16), wout_ref[A_WIDTH:, :])
        xm_ref[...] = x + out
        yield

    def run(*parts):
        live = list(parts)
        while live:
            for part in list(live):
                if next(part, StopIteration) is StopIteration:
                    live.remove(part)

    @pl.when(t == 0)
    def _():
        run(mixer())

    @pl.when((t > 0) & (t < nsteps - 1))
    def _():
        run(mlp(), mixer())

    @pl.when(t == nsteps - 1)
    def _():
        run(mlp())


def _mixer_mlp_call(x2d, seq, g, win_stack, even_index, spw, spb, vg, vb, cw, cb, cng, cnb, wout_stack,
                    g2, w1_stack, w2_stack, layer):
    n, d = x2d.shape
    nblk = n // TM
    per = TM // HALO
    const2 = lambda t: (0, 0)
    const3 = lambda t: (0, 0, 0)
    cur = lambda t: jnp.minimum(t, nblk - 1)
    resident = dict(pipeline_mode=pl.Buffered(1))
    return pl.pallas_call(
        functools.partial(_mixer_mlp_kernel, blocks_per_seq=seq // TM),
        out_shape=jax.ShapeDtypeStruct(x2d.shape, F32),
        grid=(nblk + 1,),
        in_specs=[
            pl.BlockSpec((HALO, d), lambda t: (jnp.maximum(cur(t) * per - 1, 0), 0)),
            pl.BlockSpec((TM, d), lambda t: (cur(t), 0)),
            pl.BlockSpec((HALO, d), lambda t: (jnp.minimum((cur(t) + 1) * per, n // HALO - 1), 0)),
            pl.BlockSpec((1, d), const2),
            _layer_block(win_stack, even_index),
            pl.BlockSpec(spw.shape, const3, **resident),
            pl.BlockSpec(spb.shape, const2, **resident),
            pl.BlockSpec((1, A_WIDTH), const2),
            pl.BlockSpec((1, A_WIDTH), const2),
            pl.BlockSpec(cw.shape, const2),
            pl.BlockSpec((1, B_WIDTH), const2),
            pl.BlockSpec((1, B_WIDTH), const2),
            pl.BlockSpec((1, B_WIDTH), const2),
            _layer_block(wout_stack, even_index),
            pl.BlockSpec((1, d), const2),
            _layer_block(w1_stack, layer),
            _layer_block(w2_stack, layer),
        ],
        out_specs=pl.BlockSpec((TM, d), lambda t: (jnp.maximum(t - 1, 0), 0)),
        scratch_shapes=[pltpu.VMEM((TM + 2 * HALO, B_WIDTH), F32),
                        pltpu.VMEM((TM + 2 * HALO - SUBLANES, B_WIDTH), F32),
                        pltpu.VMEM((TM, D_MODEL), F32)],
        compiler_params=_params(("arbitrary",)),
        name="mixer_ab_mlp",
    )(x2d, x2d, x2d, g, win_stack, spw, spb, vg, vb, cw, cb, cng, cnb, wout_stack, g2, w1_stack, w2_stack)


def _qkv_kernel(x_ref, g_ref, w_ref, qg_ref, kg_ref, cos_ref, sin_ref, q_ref, k_ref, v_ref, raw_ref):
    i = pl.program_id(0)
    nsteps = pl.num_programs(0)

    def project():
        h = _rms(x_ref[...], g_ref[...]).astype(BF16)
        raw_ref[...] = _dot(h, w_ref[...])

    def finish():
        lane = lax.broadcasted_iota(jnp.int32, (TM, LANES), 1)
        low = (lane % HEAD_DIM) < ROT_DIM // 2
        cos = cos_ref[...]
        sin = sin_ref[...]
        hr = lax.broadcasted_iota(jnp.int32, (MXU_COLS, MXU_COLS), 0) // HEAD_DIM
        hc = lax.broadcasted_iota(jnp.int32, (MXU_COLS, MXU_COLS), 1) // HEAD_DIM
        same_head = jnp.where(hr == hc, 1.0, 0.0).astype(BF16)

        def head_norm_rope(t4, gain):
            ms = _dot((t4 * t4).astype(BF16), same_head) * (1.0 / HEAD_DIM)
            outs = []
            for half in range(MXU_COLS // LANES):
                sl = slice(half * LANES, (half + 1) * LANES)
                t = t4[:, sl] * lax.rsqrt(ms[:, sl] + EPS) * gain
                partner = jnp.where(low, pltpu.roll(t, LANES - ROT_DIM // 2, 1), pltpu.roll(t, ROT_DIM // 2, 1))
                outs.append(t * cos + partner * sin)
            return jnp.concatenate(outs, axis=1)

        for j in range(D_MODEL // MXU_COLS):
            cols = slice(j * MXU_COLS, (j + 1) * MXU_COLS)
            q_ref[:, cols] = head_norm_rope(raw_ref[:, cols], qg_ref[...]) * (HEAD_DIM ** -0.5)
            kcols = slice(D_MODEL + j * MXU_COLS, D_MODEL + (j + 1) * MXU_COLS)
            k_ref[:, cols] = head_norm_rope(raw_ref[:, kcols], kg_ref[...])
        v_ref[...] = raw_ref[:, 2 * D_MODEL:]

    @pl.when(i == 0)
    def _():
        project()

    @pl.when((i > 0) & (i < nsteps - 1))
    def _():
        finish()
        project()

    @pl.when(i == nsteps - 1)
    def _():
        finish()


def _qkv_call(x2d, g, w_stack, odd_index, qg, kg, cos_t, sin_t, seq):
    n = x2d.shape[0]
    nblk = n // TM
    pos_blocks = seq // TM
    const = lambda i: (0, 0)
    lagged = lambda i: (jnp.maximum(i - 1, 0), 0)
    lagged_pos = lambda i: (jnp.maximum(i - 1, 0) % pos_blocks, 0)
    out = jax.ShapeDtypeStruct(x2d.shape, F32)
    return pl.pallas_call(
        _qkv_kernel,
        out_shape=(out, out, out),
        grid=(nblk + 1,),
        in_specs=[
            pl.BlockSpec((TM, D_MODEL), lambda i: (jnp.minimum(i, nblk - 1), 0)),
            pl.BlockSpec((1, D_MODEL), const),
            _layer_block(w_stack, odd_index),
            pl.BlockSpec((1, LANES), const),
            pl.BlockSpec((1, LANES), const),
            pl.BlockSpec((TM, LANES), lagged_pos),
            pl.BlockSpec((TM, LANES), lagged_pos),
        ],
        out_specs=(pl.BlockSpec((TM, D_MODEL), lagged),) * 3,
        scratch_shapes=[pltpu.VMEM((TM, 3 * D_MODEL), F32)],
        compiler_params=_params(("arbitrary",)),
        name="qkv",
    )(x2d, g, w_stack, qg, kg, cos_t, sin_t)


def _attn_kernel(q_ref, k_ref, v_ref, o_ref, qs, ks, vs, oacc, macc, lacc, bias_ref,
                 s_scr, p_scr, m_scr, l_scr, stage_q, stage_k, stage_v, *, seq):
    lane = lax.broadcasted_iota(jnp.int32, (QBLK, LANES), 1)
    head0 = lane < HEAD_DIM
    qi = lax.broadcasted_iota(jnp.int32, (QBLK, KWIN), 0)
    kj = lax.broadcasted_iota(jnp.int32, (QBLK, KWIN), 1)
    band = (kj >= qi) & (kj <= qi + 2 * HALF)
    bias_ref[0] = jnp.where(band, 0.0, NEG)
    bias_ref[1] = jnp.where(band & (kj >= HALF), 0.0, NEG)
    bias_ref[2] = jnp.where(band & (kj < KWIN - HALF), 0.0, NEG)
    zeros_pad = jnp.zeros((HALF, LANES), BF16)

    for d in DILATIONS:
        length = seq // d
        seg = length + 2 * HALF
        nblk = length // QBLK

        def regroup(idx, carry, d=d, seg=seg, nblk=nblk):
            r = idx // nblk
            c = idx % nblk
            src = pl.ds(r + c * (QBLK * d), QBLK, stride=d) if d > 1 else pl.ds(pl.multiple_of(c * QBLK, QBLK), QBLK)
            dst = pl.ds(pl.multiple_of(r * seg + HALF + c * QBLK, HALF), QBLK)
            qs[dst, :] = q_ref[0, src, :].astype(BF16)
            ks[dst, :] = k_ref[0, src, :].astype(BF16)
            vs[dst, :] = v_ref[0, src, :].astype(BF16)
            return carry

        def pads(r, carry, seg=seg, length=length):
            lo = pl.ds(pl.multiple_of(r * seg, HALF), HALF)
            hi = pl.ds(pl.multiple_of(r * seg + HALF + length, HALF), HALF)
            for ref in (ks, vs):
                ref[lo, :] = zeros_pad
                ref[hi, :] = zeros_pad
            return carry

        def regroup_two_hops(c, carry, d=d, seg=seg):
            hop = REGROUP_HOP
            span = QBLK * d
            per = span // hop
            for src_ref, stage, dst_ref in ((q_ref, stage_q, qs), (k_ref, stage_k, ks), (v_ref, stage_v, vs)):
                for r1 in range(hop):
                    for part in range(per // QBLK):
                        rows = pl.ds(c * span + r1 + part * QBLK * hop, QBLK, stride=hop)
                        stage[pl.ds(r1 * per + part * QBLK, QBLK), :] = src_ref[0, rows, :]
                for r in range(d):
                    r1, r2 = r % hop, r // hop
                    val = stage[pl.ds(r1 * per + r2, QBLK, stride=d // hop), :]
                    dst = pl.ds(pl.multiple_of(r * seg + HALF + c * QBLK, HALF), QBLK)
                    dst_ref[dst, :] = val.astype(BF16)
            return carry

        if d == REGROUP_HOP * REGROUP_HOP:
            lax.fori_loop(0, nblk, regroup_two_hops, 0)
        else:
            lax.fori_loop(0, d * nblk, regroup, 0)
        lax.fori_loop(0, d, pads, 0)

        def locate(idx, seg=seg, nblk=nblk):
            r = idx // nblk
            n = idx % nblk
            return r, n, pl.multiple_of(r * seg + n * QBLK, HALF)

        def stage_scores(g, slot, nblk=nblk):
            for u in range(ATTN_GROUP):
                _, n, base = locate(g * ATTN_GROUP + u)
                qb = qs[pl.ds(base + HALF, QBLK), :]
                kb = ks[pl.ds(base, KWIN), :]
                bias = bias_ref[jnp.where(n == 0, 1, jnp.where(n == nblk - 1, 2, 0))]
                zero = jnp.zeros_like(qb)
                qq = jnp.concatenate([jnp.where(head0, qb, zero), jnp.where(head0, zero, qb)], axis=0)
                s = lax.dot_general(qq, kb, (((1,), (1,)), ((), ())), preferred_element_type=F32)
                s_scr[slot, u, :QBLK, :] = s[:QBLK] + bias
                s_scr[slot, u, QBLK:, :] = s[QBLK:] + bias

        def stage_softmax(slot):
            for u in range(ATTN_GROUP):
                s = s_scr[slot, u]
                m = jnp.max(s, axis=-1, keepdims=True)
                p = jnp.exp(s - m)
                l = jnp.sum(p, axis=-1, keepdims=True)
                p_scr[slot, u] = p.astype(BF16)
                m_scr[slot, u] = jnp.where(head0, m[:QBLK], m[QBLK:])
                l_scr[slot, u] = jnp.where(head0, l[:QBLK], l[QBLK:])

        def stage_output(g, slot, d=d):
            results = []
            for u in range(ATTN_GROUP):
                r, n, base = locate(g * ATTN_GROUP + u)
                pv = _dot(p_scr[slot, u], vs[pl.ds(base, KWIN), :])
                acc_new = jnp.where(head0, pv[:QBLK], pv[QBLK:])
                m_new = m_scr[slot, u]
                l_new = l_scr[slot, u]
                if d > 1:
                    rows = pl.ds(r + n * (QBLK * d), QBLK, stride=d)
                else:
                    rows = pl.ds(pl.multiple_of(n * QBLK, QBLK), QBLK)
                if d == DILATIONS[0]:
                    results.append((rows, acc_new, m_new, l_new))
                    continue
                m_old = macc[rows, :]
                m_tot = jnp.maximum(m_old, m_new)
                w_old = jnp.exp(m_old - m_tot)
                w_new = jnp.exp(m_new - m_tot)
                acc_tot = oacc[rows, :] * w_old + acc_new * w_new
                l_tot = lacc[rows, :] * w_old + l_new * w_new
                if d == DILATIONS[-1]:
                    results.append((rows, acc_tot / l_tot, None, None))
                else:
                    results.append((rows, acc_tot, m_tot, l_tot))
            for rows, acc_val, m_val, l_val in results:
                if d == DILATIONS[-1]:
                    o_ref[0, rows, :] = acc_val.astype(o_ref.dtype)
                else:
                    oacc[rows, :] = acc_val
                    macc[rows, :] = m_val
                    lacc[rows, :] = l_val

        ngroups = d * nblk // ATTN_GROUP
        stage_scores(0, 0)
        stage_scores(1, 1)
        stage_softmax(0)

        def steady(t, carry):
            g = 2 * t + 2
            stage_scores(g, 0)
            stage_softmax(1)
            stage_output(g - 2, 0)
            stage_scores(g + 1, 1)
            stage_softmax(0)
            stage_output(g - 1, 1)
            return carry

        lax.fori_loop(0, (ngroups - 2) // 2, steady, 0)
        stage_softmax(1)
        stage_output(ngroups - 2, 0)
        stage_output(ngroups - 1, 1)


def _attn_call(q, k, v):
    b, s, d = q.shape
    seg_rows = s + 2 * HALF * max(DILATIONS)
    spec = pl.BlockSpec((1, s, LANES), lambda bi, hp: (bi, 0, hp))
    return pl.pallas_call(
        functools.partial(_attn_kernel, seq=s),
        out_shape=jax.ShapeDtypeStruct((b, s, d), BF16),
        grid=(b, d // LANES),
        in_specs=[spec, spec, spec],
        out_specs=spec,
        scratch_shapes=[
            pltpu.VMEM((seg_rows, LANES), BF16),
            pltpu.VMEM((seg_rows, LANES), BF16),
            pltpu.VMEM((seg_rows, LANES), BF16),
            pltpu.VMEM((s, LANES), F32),
            pltpu.VMEM((s, LANES), F32),
            pltpu.VMEM((s, LANES), F32),
            pltpu.VMEM((3, QBLK, KWIN), F32),
            pltpu.VMEM((2, ATTN_GROUP, 2 * QBLK, KWIN), F32),
            pltpu.VMEM((2, ATTN_GROUP, 2 * QBLK, KWIN), BF16),
            pltpu.VMEM((2, ATTN_GROUP, QBLK, LANES), F32),
            pltpu.VMEM((2, ATTN_GROUP, QBLK, LANES), F32),
            pltpu.VMEM((QBLK * max(DILATIONS), LANES), F32),
            pltpu.VMEM((QBLK * max(DILATIONS), LANES), F32),
            pltpu.VMEM((QBLK * max(DILATIONS), LANES), F32),
        ],
        compiler_params=_params(("arbitrary", "arbitrary")),
        name="dilated_attn",
    )(q, k, v)


def _rope_tables(seq):
    pos = jnp.arange(seq, dtype=F32)
    inv_freq = ROPE_THETA ** (-jnp.arange(0, ROT_DIM, 2, dtype=F32) / ROT_DIM)
    ang = pos[:, None] * inv_freq[None, :]
    cos, sin = jnp.cos(ang), jnp.sin(ang)
    half = ROT_DIM // 2
    l64 = jnp.arange(LANES) % HEAD_DIM
    f = l64 % half
    cos_t = jnp.where(l64 < ROT_DIM, cos[:, f], 1.0)
    sin_t = jnp.where(l64 < half, -sin[:, f], jnp.where(l64 < ROT_DIM, sin[:, f], 0.0))
    return cos_t, sin_t


def kernel(x, mix_norm_g, mlp_norm_g, mlp_w1, mlp_w2, ab_w_in, a_spatial_w, a_spatial_b, a_vnorm_g, a_vnorm_b, b_conv_w, b_conv_b, b_norm_g, b_norm_b, ab_w_out, c_w_qkv, c_q_norm_g, c_k_norm_g, c_w_out):
    b, s, d = x.shape
    depth = mix_norm_g.shape[0]
    cos_t, sin_t = _rope_tables(s)
    w1s, w2s = mlp_w1.astype(BF16), mlp_w2.astype(BF16)
    wins, wouts = ab_w_in.astype(BF16), ab_w_out.astype(BF16)
    wqkvs, wos = c_w_qkv.astype(BF16), c_w_out.astype(BF16)
    for layer in range(depth):
        i = layer // 2
        g_mix = mix_norm_g[layer][None, :]
        g_mlp = mlp_norm_g[layer][None, :]
        if layer % 2 == 0:
            spw = a_spatial_w[i].astype(BF16).reshape(A_GROUPS // 2, 2 * CHUNK, CHUNK)
            spb = jnp.repeat(a_spatial_b[i].T, A_WIDTH // A_GROUPS, axis=1)
            x = _mixer_mlp_call(
                x.reshape(b * s, d), s, g_mix, wins, i, spw, spb,
                a_vnorm_g[i][None, :], a_vnorm_b[i][None, :], b_conv_w[i], b_conv_b[i][None, :],
                b_norm_g[i][None, :], b_norm_b[i][None, :], wouts, g_mlp, w1s, w2s, layer).reshape(b, s, d)
        else:
            qg = jnp.tile(c_q_norm_g[i], LANES // HEAD_DIM)[None, :]
            kg = jnp.tile(c_k_norm_g[i], LANES // HEAD_DIM)[None, :]
            q, k, v = _qkv_call(x.reshape(b * s, d), g_mix, wqkvs, i, qg, kg, cos_t, sin_t, s)
            a = _attn_call(q.reshape(b, s, d), k.reshape(b, s, d), v.reshape(b, s, d))
            x = _proj_mlp_call(x.reshape(b * s, d), a.reshape(b * s, d), wos, i, g_mlp, w1s, w2s,
                               layer).reshape(b, s, d)
    return x
```

```python
import functools

import jax
import jax.numpy as jnp
from jax import lax
from jax.experimental import pallas as pl
from jax.experimental.pallas import tpu as pltpu

F32 = jnp.float32
BF16 = jnp.bfloat16

D_MODEL = 1024
D_FF = 4 * D_MODEL
A_WIDTH = D_MODEL // 2
B_WIDTH = D_MODEL // 2
A_GROUPS = 8
CHUNK = 128
CONV_WIDTH = 31
HEAD_DIM = 64
ROT_DIM = HEAD_DIM // 4
ROPE_THETA = 500000.0
DILATIONS = (16, 4, 1)
HALF = 64
EPS = 1e-6
NEG = -1e30

LANES = 128
SUBLANES = 8
MXU_COLS = 256
HALO = 16
TM = 512
FF_CHUNK = 512
QBLK = 128
KWIN = 2 * QBLK
ATTN_GROUP = 2
RESIDUES = max(DILATIONS)
VMEM_LIMIT = 56 * 1024 * 1024


def _params(sem, vmem=VMEM_LIMIT):
    return pltpu.CompilerParams(dimension_semantics=sem, vmem_limit_bytes=vmem)


def _rms(x, g):
    ms = jnp.mean(x * x, axis=-1, keepdims=True)
    return x * lax.rsqrt(ms + EPS) * g


def _layernorm(x, g, b):
    mu = jnp.mean(x, axis=-1, keepdims=True)
    xc = x - mu
    var = jnp.mean(xc * xc, axis=-1, keepdims=True)
    return xc * lax.rsqrt(var + EPS) * g + b


def _dot(a, b):
    return jnp.dot(a, b, preferred_element_type=F32)


def _layer_block(stack, index):
    zeros = (0,) * (stack.ndim - 1)
    return pl.BlockSpec((None,) + stack.shape[1:], lambda *_: (index,) + zeros, pipeline_mode=pl.Buffered(1))


def _mlp_steps(x, g, w1_ref, w2_ref):
    h = _rms(x, g).astype(BF16)
    acc = x
    for j in range(D_FF // FF_CHUNK):
        cols = slice(j * FF_CHUNK, (j + 1) * FF_CHUNK)
        a = _dot(h, w1_ref[:, cols])
        a = jnp.square(jnp.maximum(a, 0.0)).astype(BF16)
        acc = acc + _dot(a, w2_ref[cols, :])
        yield acc


def _mlp_body(x, g, w1_ref, w2_ref):
    for acc in _mlp_steps(x, g, w1_ref, w2_ref):
        pass
    return acc


def _proj_mlp_kernel(x_ref, a_ref, wo_ref, g_ref, w1_ref, w2_ref, o_ref):
    x = x_ref[...] + _dot(a_ref[...].astype(BF16), wo_ref[...])
    o_ref[...] = _mlp_body(x, g_ref[...], w1_ref, w2_ref)


def _proj_mlp_call(xr, a2d, wo_stack, odd_index, g, w1_stack, w2_stack, layer):
    n = a2d.shape[0]
    const = lambda i: (0, 0)
    row = lambda i: (i, 0)
    by_residue = lambda i: (i // RESIDUES, i % RESIDUES)
    return pl.pallas_call(
        _proj_mlp_kernel,
        out_shape=jax.ShapeDtypeStruct(xr.shape, F32),
        grid=(n // TM,),
        in_specs=[
            pl.BlockSpec((TM, D_MODEL), by_residue),
            pl.BlockSpec((TM, D_MODEL), row),
            _layer_block(wo_stack, odd_index),
            pl.BlockSpec((1, D_MODEL), const),
            _layer_block(w1_stack, layer),
            _layer_block(w2_stack, layer),
        ],
        out_specs=pl.BlockSpec((TM, D_MODEL), by_residue),
        compiler_params=_params(("arbitrary",)),
        name="attn_out_proj_mlp",
    )(xr, a2d, wo_stack, g, w1_stack, w2_stack)


def _mixer_mlp_kernel(xp_ref, x_ref, xn_ref, g_ref, win_ref, spw_ref, spb_ref, vg_ref, vb_ref,
                      cw_ref, cb_ref, cng_ref, cnb_ref, wout_ref, g2_ref, w1_ref, w2_ref,
                      o_ref, gs_ref, sh_ref, xm_ref, *, blocks_per_seq):
    t = pl.program_id(0)
    nsteps = pl.num_programs(0)

    def mlp():
        for acc in _mlp_steps(xm_ref[...], g2_ref[...], w1_ref, w2_ref):
            yield
        o_ref[...] = acc
        yield

    def mixer():
        i = t % blocks_per_seq
        last = blocks_per_seq - 1
        g = g_ref[...]
        x = x_ref[...]
        h = _rms(x, g).astype(BF16)
        h_all = jnp.concatenate(
            [_rms(xp_ref[...], g).astype(BF16), h, _rms(xn_ref[...], g).astype(BF16)], axis=0)

        zb = _dot(h_all, win_ref[:, 2 * A_WIDTH:])
        glu = zb[:, :B_WIDTH] * jax.nn.sigmoid(zb[:, B_WIDTH:])
        row = lax.broadcasted_iota(jnp.int32, glu.shape, 0)
        inside = ((row >= HALO) | (i > 0)) & ((row < TM + HALO) | (i < last))
        gs_ref[...] = jnp.where(inside, glu, 0.0)
        yield
        conv = jnp.broadcast_to(cb_ref[...], (TM, B_WIDTH))
        first_tap = HALO - CONV_WIDTH // 2
        for shift in range(SUBLANES):
            src = gs_ref
            if shift:
                sh_ref[...] = gs_ref[pl.ds(shift, TM + 2 * HALO - SUBLANES), :]
                src = sh_ref
            for base in range(0, 2 * HALO, SUBLANES):
                k = base + shift - first_tap
                if 0 <= k < CONV_WIDTH:
                    conv = conv + cw_ref[k:k + 1, :] * src[pl.ds(base, TM), :]
            if shift % 2 == 1 and shift < SUBLANES - 1:
                yield
        yb = _layernorm(conv, cng_ref[...], cnb_ref[...])
        yb = yb * jax.nn.sigmoid(yb)
        yield

        za = jax.nn.gelu(_dot(h, win_ref[:, :2 * A_WIDTH]))
        u = za[:, :A_WIDTH]
        v = _layernorm(za[:, A_WIDTH:], vg_ref[...], vb_ref[...]).astype(BF16)
        yield
        lane = lax.broadcasted_iota(jnp.int32, (CHUNK, LANES), 1)
        first_group = lane < LANES // 2
        ya_chunks = []
        for c in range(TM // CHUNK):
            vc = v[c * CHUNK:(c + 1) * CHUNK, :]
            parts = []
            for j in range(A_WIDTH // LANES):
                r = _dot(spw_ref[j], vc[:, j * LANES:(j + 1) * LANES])
                parts.append(jnp.where(first_group, r[:CHUNK], r[CHUNK:]))
            sv = jnp.concatenate(parts, axis=1) + spb_ref[...]
            ya_chunks.append(u[c * CHUNK:(c + 1) * CHUNK, :] * sv)
            if c % 2 == 1:
                yield
        ya = jnp.concatenate(ya_chunks, axis=0)

        out = _dot(ya.astype(BF16), wout_ref[:A_WIDTH, :]) + _dot(yb.astype(BF16), wout_ref[A_WIDTH:, :])
        xm_ref[...] = x + out
        yield

    def run(*parts):
        live = list(parts)
        while live:
            for part in list(live):
                if next(part, StopIteration) is StopIteration:
                    live.remove(part)

    @pl.when(t == 0)
    def _():
        run(mixer())

    @pl.when((t > 0) & (t < nsteps - 1))
    def _():
        run(mlp(), mixer())

    @pl.when(t == nsteps - 1)
    def _():
        run(mlp())


def _mixer_mlp_call(x2d, seq, g, win_stack, even_index, spw, spb, vg, vb, cw, cb, cng, cnb, wout_stack,
                    g2, w1_stack, w2_stack, layer):
    n, d = x2d.shape
    nblk = n // TM
    per = TM // HALO
    const2 = lambda t: (0, 0)
    const3 = lambda t: (0, 0, 0)
    cur = lambda t: jnp.minimum(t, nblk - 1)
    resident = dict(pipeline_mode=pl.Buffered(1))
    return pl.pallas_call(
        functools.partial(_mixer_mlp_kernel, blocks_per_seq=seq // TM),
        out_shape=jax.ShapeDtypeStruct(x2d.shape, F32),
        grid=(nblk + 1,),
        in_specs=[
            pl.BlockSpec((HALO, d), lambda t: (jnp.maximum(cur(t) * per - 1, 0), 0)),
            pl.BlockSpec((TM, d), lambda t: (cur(t), 0)),
            pl.BlockSpec((HALO, d), lambda t: (jnp.minimum((cur(t) + 1) * per, n // HALO - 1), 0)),
            pl.BlockSpec((1, d), const2),
            _layer_block(win_stack, even_index),
            pl.BlockSpec(spw.shape, const3, **resident),
            pl.BlockSpec(spb.shape, const2, **resident),
            pl.BlockSpec((1, A_WIDTH), const2),
            pl.BlockSpec((1, A_WIDTH), const2),
            pl.BlockSpec(cw.shape, const2),
            pl.BlockSpec((1, B_WIDTH), const2),
            pl.BlockSpec((1, B_WIDTH), const2),
            pl.BlockSpec((1, B_WIDTH), const2),
            _layer_block(wout_stack, even_index),
            pl.BlockSpec((1, d), const2),
            _layer_block(w1_stack, layer),
            _layer_block(w2_stack, layer),
        ],
        out_specs=pl.BlockSpec((TM, d), lambda t: (jnp.maximum(t - 1, 0), 0)),
        scratch_shapes=[pltpu.VMEM((TM + 2 * HALO, B_WIDTH), F32),
                        pltpu.VMEM((TM + 2 * HALO - SUBLANES, B_WIDTH), F32),
                        pltpu.VMEM((TM, D_MODEL), F32)],
        compiler_params=_params(("arbitrary",)),
        name="mixer_ab_mlp",
    )(x2d, x2d, x2d, g, win_stack, spw, spb, vg, vb, cw, cb, cng, cnb, wout_stack, g2, w1_stack, w2_stack)


def _qkv_kernel(x_ref, g_ref, w_ref, qg_ref, kg_ref, cos_ref, sin_ref,
                q_ref, k_ref, v_ref, kn_ref, vn_ref, raw_ref):
    i = pl.program_id(0)
    nsteps = pl.num_programs(0)

    def project():
        h = _rms(x_ref[...], g_ref[...]).astype(BF16)
        raw_ref[...] = _dot(h, w_ref[...])

    def finish():
        lane = lax.broadcasted_iota(jnp.int32, (TM, LANES), 1)
        low = (lane % HEAD_DIM) < ROT_DIM // 2
        cos = cos_ref[...]
        sin = sin_ref[...]
        hr = lax.broadcasted_iota(jnp.int32, (MXU_COLS, MXU_COLS), 0) // HEAD_DIM
        hc = lax.broadcasted_iota(jnp.int32, (MXU_COLS, MXU_COLS), 1) // HEAD_DIM
        same_head = jnp.where(hr == hc, 1.0, 0.0).astype(BF16)

        def head_norm_rope(t4, gain):
            ms = _dot((t4 * t4).astype(BF16), same_head) * (1.0 / HEAD_DIM)
            outs = []
            for half in range(MXU_COLS // LANES):
                sl = slice(half * LANES, (half + 1) * LANES)
                t = t4[:, sl] * lax.rsqrt(ms[:, sl] + EPS) * gain
                partner = jnp.where(low, pltpu.roll(t, LANES - ROT_DIM // 2, 1), pltpu.roll(t, ROT_DIM // 2, 1))
                outs.append(t * cos + partner * sin)
            return jnp.concatenate(outs, axis=1)

        for j in range(D_MODEL // MXU_COLS):
            cols = slice(j * MXU_COLS, (j + 1) * MXU_COLS)
            q_ref[:, cols] = head_norm_rope(raw_ref[:, cols], qg_ref[...]) * (HEAD_DIM ** -0.5)
            kcols = slice(D_MODEL + j * MXU_COLS, D_MODEL + (j + 1) * MXU_COLS)
            k = head_norm_rope(raw_ref[:, kcols], kg_ref[...]).astype(BF16)
            k_ref[:, cols] = k
            kn_ref[:, cols] = k
        v = raw_ref[:, 2 * D_MODEL:].astype(BF16)
        v_ref[...] = v
        vn_ref[...] = v

    @pl.when(i == 0)
    def _():
        project()

    @pl.when((i > 0) & (i < nsteps - 1))
    def _():
        finish()
        project()

    @pl.when(i == nsteps - 1)
    def _():
        finish()


def _qkv_call(xr, g, w_stack, odd_index, qg, kg, cos_r, sin_r):
    rows = xr.shape[0]
    nblk = rows // TM * RESIDUES
    n = nblk * TM
    const = lambda i: (0, 0)
    lag = lambda i: jnp.maximum(i - 1, 0)
    cur = lambda i: jnp.minimum(i, nblk - 1)
    lagged = lambda i: (lag(i), 0)
    lagged_by_residue = lambda i: (lag(i) // RESIDUES, lag(i) % RESIDUES)
    lagged_residue = lambda i: (lag(i) % RESIDUES, 0)
    return pl.pallas_call(
        _qkv_kernel,
        out_shape=(jax.ShapeDtypeStruct((n, D_MODEL), F32),
                   jax.ShapeDtypeStruct((n, D_MODEL), BF16),
                   jax.ShapeDtypeStruct((n, D_MODEL), BF16),
                   jax.ShapeDtypeStruct(xr.shape, BF16),
                   jax.ShapeDtypeStruct(xr.shape, BF16)),
        grid=(nblk + 1,),
        in_specs=[
            pl.BlockSpec((TM, D_MODEL), lambda i: (cur(i) // RESIDUES, cur(i) % RESIDUES)),
            pl.BlockSpec((1, D_MODEL), const),
            _layer_block(w_stack, odd_index),
            pl.BlockSpec((1, LANES), const),
            pl.BlockSpec((1, LANES), const),
            pl.BlockSpec((TM, LANES), lagged_residue),
            pl.BlockSpec((TM, LANES), lagged_residue),
        ],
        out_specs=(pl.BlockSpec((TM, D_MODEL), lagged),) * 3 + (pl.BlockSpec((TM, D_MODEL), lagged_by_residue),) * 2,
        scratch_shapes=[pltpu.VMEM((TM, 3 * D_MODEL), F32)],
        compiler_params=_params(("arbitrary",)),
        name="qkv",
    )(xr, g, w_stack, qg, kg, cos_r, sin_r)


def _attn_kernel(q_ref, k_ref, v_ref, kn_ref, vn_ref, o_ref, oacc, macc, lacc, bias_ref,
                 s_scr, p_scr, m_scr, l_scr, *, seq):
    per = seq // RESIDUES
    lane = lax.broadcasted_iota(jnp.int32, (QBLK, LANES), 1)
    head0 = lane < HEAD_DIM
    row = lax.broadcasted_iota(jnp.int32, (QBLK, KWIN), 0)
    col = lax.broadcasted_iota(jnp.int32, (QBLK, KWIN), 1)
    for pat, d in enumerate(DILATIONS):
        nslab = RESIDUES // d
        qp, kp = QBLK // nslab, KWIN // nslab
        qpos = nslab * (row % qp) + row // qp
        kpos = col if d == 1 else nslab * (col % kp) + col // kp
        for which, shift in enumerate((HALF, 0, 2 * HALF)):
            diff = kpos - qpos - shift
            bias_ref[3 * pat + which] = jnp.where((diff >= -HALF) & (diff <= HALF), 0.0, NEG)

    for pat, d in enumerate(DILATIONS):
        nslab = RESIDUES // d
        qp, kp = QBLK // nslab, KWIN // nslab
        nblk = seq // d // QBLK

        def locate(idx, nblk=nblk):
            return idx // nblk, idx % nblk

        def slab_rows(seg, start, size, d=d, nslab=nslab):
            return [(seg + d * j, pl.ds(start, size)) for j in range(nslab)]

        def gather(ref, pieces):
            parts = [ref[0, slab, rows, :] for slab, rows in pieces]
            return parts[0] if len(parts) == 1 else jnp.concatenate(parts, axis=0)

        def query_pieces(seg, n, qp=qp):
            return slab_rows(seg, pl.multiple_of(n * qp, qp), qp)

        def key_block(ref, ref_n, seg, n, d=d, qp=qp, kp=kp):
            if d == 1:
                start = jnp.clip(n * QBLK - HALF, 0, seq - KWIN)
                return ref_n[0, pl.ds(pl.multiple_of(start, HALF), KWIN), :]
            start = jnp.clip(n * qp - kp // 4, 0, per - kp)
            return gather(ref, slab_rows(seg, pl.multiple_of(start, kp // 4), kp))

        def stage_scores(g, slot, pat=pat, nblk=nblk):
            for u in range(ATTN_GROUP):
                seg, n = locate(g * ATTN_GROUP + u)
                qb = gather(q_ref, query_pieces(seg, n)).astype(BF16)
                kb = key_block(k_ref, kn_ref, seg, n)
                bias = bias_ref[3 * pat + jnp.where(n == 0, 1, jnp.where(n == nblk - 1, 2, 0))]
                zero = jnp.zeros_like(qb)
                qq = jnp.concatenate([jnp.where(head0, qb, zero), jnp.where(head0, zero, qb)], axis=0)
                s = lax.dot_general(qq, kb, (((1,), (1,)), ((), ())), preferred_element_type=F32)
                s_scr[slot, u, :QBLK, :] = s[:QBLK] + bias
                s_scr[slot, u, QBLK:, :] = s[QBLK:] + bias

        def stage_softmax(slot):
            for u in range(ATTN_GROUP):
                s = s_scr[slot, u]
                m = jnp.max(s, axis=-1, keepdims=True)
                p = jnp.exp(s - m)
                l = jnp.sum(p, axis=-1, keepdims=True)
                p_scr[slot, u] = p.astype(BF16)
                m_scr[slot, u] = jnp.where(head0, m[:QBLK], m[QBLK:])
                l_scr[slot, u] = jnp.where(head0, l[:QBLK], l[QBLK:])

        def stage_output(g, slot, d=d, qp=qp):
            results = []
            for u in range(ATTN_GROUP):
                seg, n = locate(g * ATTN_GROUP + u)
                pv = _dot(p_scr[slot, u], key_block(v_ref, vn_ref, seg, n))
                acc_new = jnp.where(head0, pv[:QBLK], pv[QBLK:])
                m_new = m_scr[slot, u]
                l_new = l_scr[slot, u]
                pieces = query_pieces(seg, n)
                if d == DILATIONS[0]:
                    results.append((pieces, acc_new, m_new, l_new))
                    continue
                m_old = jnp.concatenate([macc[slab, rows, :] for slab, rows in pieces], axis=0)
                acc_old = jnp.concatenate([oacc[slab, rows, :] for slab, rows in pieces], axis=0)
                l_old = jnp.concatenate([lacc[slab, rows, :] for slab, rows in pieces], axis=0)
                m_tot = jnp.maximum(m_old, m_new)
                w_old = jnp.exp(m_old - m_tot)
                w_new = jnp.exp(m_new - m_tot)
                acc_tot = acc_old * w_old + acc_new * w_new
                l_tot = l_old * w_old + l_new * w_new
                if d == DILATIONS[-1]:
                    results.append((pieces, acc_tot / l_tot, None, None))
                else:
                    results.append((pieces, acc_tot, m_tot, l_tot))
            for pieces, acc_val, m_val, l_val in results:
                for k, (slab, rows) in enumerate(pieces):
                    part = slice(k * qp, (k + 1) * qp)
                    if d == DILATIONS[-1]:
                        o_ref[0, slab, rows, :] = acc_val[part]
                    else:
                        oacc[slab, rows, :] = acc_val[part]
                        macc[slab, rows, :] = m_val[part]
                        lacc[slab, rows, :] = l_val[part]

        ngroups = d * nblk // ATTN_GROUP
        stage_scores(0, 0)
        stage_scores(1, 1)
        stage_softmax(0)

        def steady(t, carry):
            g = 2 * t + 2
            stage_scores(g, 0)
            stage_softmax(1)
            stage_output(g - 2, 0)
            stage_scores(g + 1, 1)
            stage_softmax(0)
            stage_output(g - 1, 1)
            return carry

        lax.fori_loop(0, (ngroups - 2) // 2, steady, 0)
        stage_softmax(1)
        stage_output(ngroups - 2, 0)
        stage_output(ngroups - 1, 1)


def _attn_call(q, k, v, kn, vn):
    b, res, per, d = q.shape
    s = res * per
    by_res = pl.BlockSpec((1, res, per, LANES), lambda bi, hp: (bi, 0, 0, hp))
    by_pos = pl.BlockSpec((1, s, LANES), lambda bi, hp: (bi, 0, hp))
    return pl.pallas_call(
        functools.partial(_attn_kernel, seq=s),
        out_shape=jax.ShapeDtypeStruct(q.shape, F32),
        grid=(b, d // LANES),
        in_specs=[by_res, by_res, by_res, by_pos, by_pos],
        out_specs=by_res,
        scratch_shapes=[
            pltpu.VMEM((res, per, LANES), F32),
            pltpu.VMEM((res, per, LANES), F32),
            pltpu.VMEM((res, per, LANES), F32),
            pltpu.VMEM((3 * len(DILATIONS), QBLK, KWIN), F32),
            pltpu.VMEM((2, ATTN_GROUP, 2 * QBLK, KWIN), F32),
            pltpu.VMEM((2, ATTN_GROUP, 2 * QBLK, KWIN), BF16),
            pltpu.VMEM((2, ATTN_GROUP, QBLK, LANES), F32),
            pltpu.VMEM((2, ATTN_GROUP, QBLK, LANES), F32),
        ],
        compiler_params=_params(("arbitrary", "arbitrary")),
        name="dilated_attn",
    )(q, k, v, kn, vn)


def _rope_tables(seq):
    pos = jnp.arange(seq, dtype=F32)
    inv_freq = ROPE_THETA ** (-jnp.arange(0, ROT_DIM, 2, dtype=F32) / ROT_DIM)
    ang = pos[:, None] * inv_freq[None, :]
    cos, sin = jnp.cos(ang), jnp.sin(ang)
    half = ROT_DIM // 2
    l64 = jnp.arange(LANES) % HEAD_DIM
    f = l64 % half
    cos_t = jnp.where(l64 < ROT_DIM, cos[:, f], 1.0)
    sin_t = jnp.where(l64 < half, -sin[:, f], jnp.where(l64 < ROT_DIM, sin[:, f], 0.0))
    by_residue = lambda t: t.reshape(seq // RESIDUES, RESIDUES, LANES).transpose(1, 0, 2).reshape(seq, LANES)
    return by_residue(cos_t), by_residue(sin_t)


def kernel(x, mix_norm_g, mlp_norm_g, mlp_w1, mlp_w2, ab_w_in, a_spatial_w, a_spatial_b, a_vnorm_g, a_vnorm_b, b_conv_w, b_conv_b, b_norm_g, b_norm_b, ab_w_out, c_w_qkv, c_q_norm_g, c_k_norm_g, c_w_out):
    b, s, d = x.shape
    depth = mix_norm_g.shape[0]
    assert s // RESIDUES == TM, "odd layers take one residue's positions per grid step"
    cos_r, sin_r = _rope_tables(s)
    w1s, w2s = mlp_w1.astype(BF16), mlp_w2.astype(BF16)
    wins, wouts = ab_w_in.astype(BF16), ab_w_out.astype(BF16)
    wqkvs, wos = c_w_qkv.astype(BF16), c_w_out.astype(BF16)
    for layer in range(depth):
        i = layer // 2
        g_mix = mix_norm_g[layer][None, :]
        g_mlp = mlp_norm_g[layer][None, :]
        if layer % 2 == 0:
            spw = a_spatial_w[i].astype(BF16).reshape(A_GROUPS // 2, 2 * CHUNK, CHUNK)
            spb = jnp.repeat(a_spatial_b[i].T, A_WIDTH // A_GROUPS, axis=1)
            x = _mixer_mlp_call(
                x.reshape(b * s, d), s, g_mix, wins, i, spw, spb,
                a_vnorm_g[i][None, :], a_vnorm_b[i][None, :], b_conv_w[i], b_conv_b[i][None, :],
                b_norm_g[i][None, :], b_norm_b[i][None, :], wouts, g_mlp, w1s, w2s, layer).reshape(b, s, d)
        else:
            qg = jnp.tile(c_q_norm_g[i], LANES // HEAD_DIM)[None, :]
            kg = jnp.tile(c_k_norm_g[i], LANES // HEAD_DIM)[None, :]
            xr = x.reshape(b * TM, RESIDUES * d)
            q, k, v, kn, vn = _qkv_call(xr, g_mix, wqkvs, i, qg, kg, cos_r, sin_r)
            by_res = (b, RESIDUES, TM, d)
            a = _attn_call(q.reshape(by_res), k.reshape(by_res), v.reshape(by_res),
                           kn.reshape(b, s, d), vn.reshape(b, s, d))
            x = _proj_mlp_call(xr, a.reshape(b * s, d), wos, i, g_mlp, w1s, w2s, layer).reshape(b, s, d)
    return x
```

```python
import functools

import jax
import jax.numpy as jnp
from jax import lax
from jax.experimental import pallas as pl
from jax.experimental.pallas import tpu as pltpu

F32 = jnp.float32
BF16 = jnp.bfloat16

D_MODEL = 1024
D_FF = 4 * D_MODEL
A_WIDTH = D_MODEL // 2
B_WIDTH = D_MODEL // 2
A_GROUPS = 8
CHUNK = 128
CONV_WIDTH = 31
HEAD_DIM = 64
ROT_DIM = HEAD_DIM // 4
ROPE_THETA = 500000.0
DILATIONS = (16, 4, 1)
HALF = 64
EPS = 1e-6
NEG = -1e30

LANES = 128
SUBLANES = 8
MXU_COLS = 256
HALO = 16
TM = 512
FF_CHUNK = 512
QBLK = 128
KWIN = 2 * QBLK
ATTN_GROUP = 2
RESIDUES = max(DILATIONS)
REGROUP_HOP = 4
VMEM_LIMIT = 56 * 1024 * 1024


def _params(sem, vmem=VMEM_LIMIT):
    return pltpu.CompilerParams(dimension_semantics=sem, vmem_limit_bytes=vmem)


def _rms(x, g):
    ms = jnp.mean(x * x, axis=-1, keepdims=True)
    return x * lax.rsqrt(ms + EPS) * g


def _layernorm(x, g, b):
    mu = jnp.mean(x, axis=-1, keepdims=True)
    xc = x - mu
    var = jnp.mean(xc * xc, axis=-1, keepdims=True)
    return xc * lax.rsqrt(var + EPS) * g + b


def _dot(a, b):
    return jnp.dot(a, b, preferred_element_type=F32)


def _layer_block(stack, index):
    zeros = (0,) * (stack.ndim - 1)
    return pl.BlockSpec((None,) + stack.shape[1:], lambda *_: (index,) + zeros, pipeline_mode=pl.Buffered(1))


def _mlp_steps(x, g, w1_ref, w2_ref):
    h = _rms(x, g).astype(BF16)
    acc = x
    for j in range(D_FF // FF_CHUNK):
        cols = slice(j * FF_CHUNK, (j + 1) * FF_CHUNK)
        a = _dot(h, w1_ref[:, cols])
        a = jnp.square(jnp.maximum(a, 0.0)).astype(BF16)
        acc = acc + _dot(a, w2_ref[cols, :])
        yield acc


def _mlp_body(x, g, w1_ref, w2_ref):
    for acc in _mlp_steps(x, g, w1_ref, w2_ref):
        pass
    return acc


def _proj_mlp_kernel(x_ref, a_ref, wo_ref, g_ref, w1_ref, w2_ref, o_ref):
    x = x_ref[...] + _dot(a_ref[...].astype(BF16), wo_ref[...])
    o_ref[...] = _mlp_body(x, g_ref[...], w1_ref, w2_ref)


def _proj_mlp_call(x2d, a2d, wo_stack, odd_index, g, w1_stack, w2_stack, layer):
    n = x2d.shape[0]
    const = lambda i: (0, 0)
    row = lambda i: (i, 0)
    return pl.pallas_call(
        _proj_mlp_kernel,
        out_shape=jax.ShapeDtypeStruct(x2d.shape, F32),
        grid=(n // TM,),
        in_specs=[
            pl.BlockSpec((TM, D_MODEL), row),
            pl.BlockSpec((TM, D_MODEL), row),
            _layer_block(wo_stack, odd_index),
            pl.BlockSpec((1, D_MODEL), const),
            _layer_block(w1_stack, layer),
            _layer_block(w2_stack, layer),
        ],
        out_specs=pl.BlockSpec((TM, D_MODEL), row),
        compiler_params=_params(("arbitrary",)),
        name="attn_out_proj_mlp",
    )(x2d, a2d, wo_stack, g, w1_stack, w2_stack)


def _mixer_mlp_kernel(xp_ref, x_ref, xn_ref, g_ref, win_ref, spw_ref, spb_ref, vg_ref, vb_ref,
                      cw_ref, cb_ref, cng_ref, cnb_ref, wout_ref, g2_ref, w1_ref, w2_ref,
                      o_ref, gs_ref, sh_ref, xm_ref, *, blocks_per_seq):
    t = pl.program_id(0)
    nsteps = pl.num_programs(0)

    def mlp():
        for acc in _mlp_steps(xm_ref[...], g2_ref[...], w1_ref, w2_ref):
            yield
        o_ref[...] = acc
        yield

    def mixer():
        i = t % blocks_per_seq
        last = blocks_per_seq - 1
        g = g_ref[...]
        x = x_ref[...]
        h = _rms(x, g).astype(BF16)
        h_all = jnp.concatenate(
            [_rms(xp_ref[...], g).astype(BF16), h, _rms(xn_ref[...], g).astype(BF16)], axis=0)

        zb = _dot(h_all, win_ref[:, 2 * A_WIDTH:])
        glu = zb[:, :B_WIDTH] * jax.nn.sigmoid(zb[:, B_WIDTH:])
        row = lax.broadcasted_iota(jnp.int32, glu.shape, 0)
        inside = ((row >= HALO) | (i > 0)) & ((row < TM + HALO) | (i < last))
        gs_ref[...] = jnp.where(inside, glu, 0.0)
        yield
        conv = jnp.broadcast_to(cb_ref[...], (TM, B_WIDTH))
        first_tap = HALO - CONV_WIDTH // 2
        for shift in range(SUBLANES):
            src = gs_ref
            if shift:
                sh_ref[...] = gs_ref[pl.ds(shift, TM + 2 * HALO - SUBLANES), :]
                src = sh_ref
            for base in range(0, 2 * HALO, SUBLANES):
                k = base + shift - first_tap
                if 0 <= k < CONV_WIDTH:
                    conv = conv + cw_ref[k:k + 1, :] * src[pl.ds(base, TM), :]
            if shift % 2 == 1 and shift < SUBLANES - 1:
                yield
        yb = _layernorm(conv, cng_ref[...], cnb_ref[...])
        yb = yb * jax.nn.sigmoid(yb)
        yield

        za = jax.nn.gelu(_dot(h, win_ref[:, :2 * A_WIDTH]))
        u = za[:, :A_WIDTH]
        v = _layernorm(za[:, A_WIDTH:], vg_ref[...], vb_ref[...]).astype(BF16)
        yield
        lane = lax.broadcasted_iota(jnp.int32, (CHUNK, LANES), 1)
        first_group = lane < LANES // 2
        ya_chunks = []
        for c in range(TM // CHUNK):
            vc = v[c * CHUNK:(c + 1) * CHUNK, :]
            parts = []
            for j in range(A_WIDTH // LANES):
                r = _dot(spw_ref[j], vc[:, j * LANES:(j + 1) * LANES])
                parts.append(jnp.where(first_group, r[:CHUNK], r[CHUNK:]))
            sv = jnp.concatenate(parts, axis=1) + spb_ref[...]
            ya_chunks.append(u[c * CHUNK:(c + 1) * CHUNK, :] * sv)
            if c % 2 == 1:
                yield
        ya = jnp.concatenate(ya_chunks, axis=0)

        out = _dot(ya.astype(BF16), wout_ref[:A_WIDTH, :]) + _dot(yb.astype(BF16), wout_ref[A_WIDTH:, :])
        xm_ref[...] = x + out
        yield

    def run(*parts):
        live = list(parts)
        while live:
            for part in list(live):
                if next(part, StopIteration) is StopIteration:
                    live.remove(part)

    @pl.when(t == 0)
    def _():
        run(mixer())

    @pl.when((t > 0) & (t < nsteps - 1))
    def _():
        run(mlp(), mixer())

    @pl.when(t == nsteps - 1)
    def _():
        run(mlp())


def _mixer_mlp_call(x2d, seq, g, win_stack, even_index, spw, spb, vg, vb, cw, cb, cng, cnb, wout_stack,
                    g2, w1_stack, w2_stack, layer):
    n, d = x2d.shape
    nblk = n // TM
    per = TM // HALO
    const2 = lambda t: (0, 0)
    const3 = lambda t: (0, 0, 0)
    cur = lambda t: jnp.minimum(t, nblk - 1)
    resident = dict(pipeline_mode=pl.Buffered(1))
    return pl.pallas_call(
        functools.partial(_mixer_mlp_kernel, blocks_per_seq=seq // TM),
        out_shape=jax.ShapeDtypeStruct(x2d.shape, F32),
        grid=(nblk + 1,),
        in_specs=[
            pl.BlockSpec((HALO, d), lambda t: (jnp.maximum(cur(t) * per - 1, 0), 0)),
            pl.BlockSpec((TM, d), lambda t: (cur(t), 0)),
            pl.BlockSpec((HALO, d), lambda t: (jnp.minimum((cur(t) + 1) * per, n // HALO - 1), 0)),
            pl.BlockSpec((1, d), const2),
            _layer_block(win_stack, even_index),
            pl.BlockSpec(spw.shape, const3, **resident),
            pl.BlockSpec(spb.shape, const2, **resident),
            pl.BlockSpec((1, A_WIDTH), const2),
            pl.BlockSpec((1, A_WIDTH), const2),
            pl.BlockSpec(cw.shape, const2),
            pl.BlockSpec((1, B_WIDTH), const2),
            pl.BlockSpec((1, B_WIDTH), const2),
            pl.BlockSpec((1, B_WIDTH), const2),
            _layer_block(wout_stack, even_index),
            pl.BlockSpec((1, d), const2),
            _layer_block(w1_stack, layer),
            _layer_block(w2_stack, layer),
        ],
        out_specs=pl.BlockSpec((TM, d), lambda t: (jnp.maximum(t - 1, 0), 0)),
        scratch_shapes=[pltpu.VMEM((TM + 2 * HALO, B_WIDTH), F32),
                        pltpu.VMEM((TM + 2 * HALO - SUBLANES, B_WIDTH), F32),
                        pltpu.VMEM((TM, D_MODEL), F32)],
        compiler_params=_params(("arbitrary",)),
        name="mixer_ab_mlp",
    )(x2d, x2d, x2d, g, win_stack, spw, spb, vg, vb, cw, cb, cng, cnb, wout_stack, g2, w1_stack, w2_stack)


def _qkv_kernel(x_ref, g_ref, w_ref, qg_ref, kg_ref, cos_ref, sin_ref, q_ref, k_ref, v_ref, raw_ref):
    i = pl.program_id(0)
    nsteps = pl.num_programs(0)

    def project():
        h = _rms(x_ref[...], g_ref[...]).astype(BF16)
        raw_ref[...] = _dot(h, w_ref[...])

    def finish():
        lane = lax.broadcasted_iota(jnp.int32, (TM, LANES), 1)
        low = (lane % HEAD_DIM) < ROT_DIM // 2
        cos = cos_ref[...]
        sin = sin_ref[...]
        hr = lax.broadcasted_iota(jnp.int32, (MXU_COLS, MXU_COLS), 0) // HEAD_DIM
        hc = lax.broadcasted_iota(jnp.int32, (MXU_COLS, MXU_COLS), 1) // HEAD_DIM
        same_head = jnp.where(hr == hc, 1.0, 0.0).astype(BF16)

        def head_norm_rope(t4, gain):
            ms = _dot((t4 * t4).astype(BF16), same_head) * (1.0 / HEAD_DIM)
            outs = []
            for half in range(MXU_COLS // LANES):
                sl = slice(half * LANES, (half + 1) * LANES)
                t = t4[:, sl] * lax.rsqrt(ms[:, sl] + EPS) * gain
                partner = jnp.where(low, pltpu.roll(t, LANES - ROT_DIM // 2, 1), pltpu.roll(t, ROT_DIM // 2, 1))
                outs.append(t * cos + partner * sin)
            return jnp.concatenate(outs, axis=1)

        for j in range(D_MODEL // MXU_COLS):
            cols = slice(j * MXU_COLS, (j + 1) * MXU_COLS)
            q_ref[:, cols] = head_norm_rope(raw_ref[:, cols], qg_ref[...]) * (HEAD_DIM ** -0.5)
            kcols = slice(D_MODEL + j * MXU_COLS, D_MODEL + (j + 1) * MXU_COLS)
            k_ref[:, cols] = head_norm_rope(raw_ref[:, kcols], kg_ref[...])
        v_ref[...] = raw_ref[:, 2 * D_MODEL:]

    @pl.when(i == 0)
    def _():
        project()

    @pl.when((i > 0) & (i < nsteps - 1))
    def _():
        finish()
        project()

    @pl.when(i == nsteps - 1)
    def _():
        finish()


def _qkv_call(x2d, g, w_stack, odd_index, qg, kg, cos_t, sin_t, seq):
    n = x2d.shape[0]
    nblk = n // TM
    pos_blocks = seq // TM
    const = lambda i: (0, 0)
    lagged = lambda i: (jnp.maximum(i - 1, 0), 0)
    lagged_pos = lambda i: (jnp.maximum(i - 1, 0) % pos_blocks, 0)
    out = jax.ShapeDtypeStruct(x2d.shape, F32)
    return pl.pallas_call(
        _qkv_kernel,
        out_shape=(out, out, out),
        grid=(nblk + 1,),
        in_specs=[
            pl.BlockSpec((TM, D_MODEL), lambda i: (jnp.minimum(i, nblk - 1), 0)),
            pl.BlockSpec((1, D_MODEL), const),
            _layer_block(w_stack, odd_index),
            pl.BlockSpec((1, LANES), const),
            pl.BlockSpec((1, LANES), const),
            pl.BlockSpec((TM, LANES), lagged_pos),
            pl.BlockSpec((TM, LANES), lagged_pos),
        ],
        out_specs=(pl.BlockSpec((TM, D_MODEL), lagged),) * 3,
        scratch_shapes=[pltpu.VMEM((TM, 3 * D_MODEL), F32)],
        compiler_params=_params(("arbitrary",)),
        name="qkv",
    )(x2d, g, w_stack, qg, kg, cos_t, sin_t)


def _attn_kernel(qn_ref, kn_ref, vn_ref, o_ref, q_ref, k_ref, v_ref, oacc, macc, lacc, bias_ref,
                 s_scr, p_scr, m_scr, l_scr, stage, *, seq):
    per = seq // RESIDUES

    def regroup(c, carry):
        hop = REGROUP_HOP
        span = QBLK * RESIDUES
        seg = span // hop
        for src_ref, dst_ref in ((qn_ref, q_ref), (kn_ref, k_ref), (vn_ref, v_ref)):
            for r1 in range(hop):
                for part in range(seg // QBLK):
                    rows = pl.ds(c * span + r1 + part * QBLK * hop, QBLK, stride=hop)
                    stage[pl.ds(r1 * seg + part * QBLK, QBLK), :] = src_ref[0, rows, :]
            for r in range(RESIDUES):
                r1, r2 = r % hop, r // hop
                val = stage[pl.ds(r1 * seg + r2, QBLK, stride=RESIDUES // hop), :]
                dst_ref[r, pl.ds(pl.multiple_of(c * QBLK, QBLK), QBLK), :] = val.astype(dst_ref.dtype)
        return carry

    lax.fori_loop(0, per // QBLK, regroup, 0)
    lane = lax.broadcasted_iota(jnp.int32, (QBLK, LANES), 1)
    head0 = lane < HEAD_DIM
    row = lax.broadcasted_iota(jnp.int32, (QBLK, KWIN), 0)
    col = lax.broadcasted_iota(jnp.int32, (QBLK, KWIN), 1)
    for pat, d in enumerate(DILATIONS):
        nslab = RESIDUES // d
        qp, kp = QBLK // nslab, KWIN // nslab
        qpos = nslab * (row % qp) + row // qp
        kpos = col if d == 1 else nslab * (col % kp) + col // kp
        for which, shift in enumerate((HALF, 0, 2 * HALF)):
            diff = kpos - qpos - shift
            bias_ref[3 * pat + which] = jnp.where((diff >= -HALF) & (diff <= HALF), 0.0, NEG)

    for pat, d in enumerate(DILATIONS):
        nslab = RESIDUES // d
        qp, kp = QBLK // nslab, KWIN // nslab
        nblk = seq // d // QBLK

        def locate(idx, nblk=nblk):
            return idx // nblk, idx % nblk

        def slab_rows(seg, start, size, d=d, nslab=nslab):
            return [(seg + d * j, pl.ds(start, size)) for j in range(nslab)]

        def gather(ref, pieces):
            parts = [ref[slab, rows, :] for slab, rows in pieces]
            return parts[0] if len(parts) == 1 else jnp.concatenate(parts, axis=0)

        def query_pieces(seg, n, qp=qp):
            return slab_rows(seg, pl.multiple_of(n * qp, qp), qp)

        def key_block(ref, ref_n, seg, n, d=d, qp=qp, kp=kp):
            if d == 1:
                start = jnp.clip(n * QBLK - HALF, 0, seq - KWIN)
                return ref_n[0, pl.ds(pl.multiple_of(start, HALF), KWIN), :].astype(BF16)
            start = jnp.clip(n * qp - kp // 4, 0, per - kp)
            return gather(ref, slab_rows(seg, pl.multiple_of(start, kp // 4), kp))

        def stage_scores(g, slot, pat=pat, nblk=nblk):
            for u in range(ATTN_GROUP):
                seg, n = locate(g * ATTN_GROUP + u)
                qb = gather(q_ref, query_pieces(seg, n)).astype(BF16)
                kb = key_block(k_ref, kn_ref, seg, n)
                bias = bias_ref[3 * pat + jnp.where(n == 0, 1, jnp.where(n == nblk - 1, 2, 0))]
                zero = jnp.zeros_like(qb)
                qq = jnp.concatenate([jnp.where(head0, qb, zero), jnp.where(head0, zero, qb)], axis=0)
                s = lax.dot_general(qq, kb, (((1,), (1,)), ((), ())), preferred_element_type=F32)
                s_scr[slot, u, :QBLK, :] = s[:QBLK] + bias
                s_scr[slot, u, QBLK:, :] = s[QBLK:] + bias

        def stage_softmax(slot):
            for u in range(ATTN_GROUP):
                s = s_scr[slot, u]
                m = jnp.max(s, axis=-1, keepdims=True)
                p = jnp.exp(s - m)
                l = jnp.sum(p, axis=-1, keepdims=True)
                p_scr[slot, u] = p.astype(BF16)
                m_scr[slot, u] = jnp.where(head0, m[:QBLK], m[QBLK:])
                l_scr[slot, u] = jnp.where(head0, l[:QBLK], l[QBLK:])

        def stage_output(g, slot, d=d, qp=qp, nslab=nslab):
            results = []
            for u in range(ATTN_GROUP):
                seg, n = locate(g * ATTN_GROUP + u)
                pv = _dot(p_scr[slot, u], key_block(v_ref, vn_ref, seg, n))
                acc_new = jnp.where(head0, pv[:QBLK], pv[QBLK:])
                m_new = m_scr[slot, u]
                l_new = l_scr[slot, u]
                pieces = query_pieces(seg, n)
                if d == DILATIONS[0]:
                    results.append((pieces, acc_new, m_new, l_new))
                    continue
                m_old = jnp.concatenate([macc[slab, rows, :] for slab, rows in pieces], axis=0)
                acc_old = jnp.concatenate([oacc[slab, rows, :] for slab, rows in pieces], axis=0)
                l_old = jnp.concatenate([lacc[slab, rows, :] for slab, rows in pieces], axis=0)
                m_tot = jnp.maximum(m_old, m_new)
                w_old = jnp.exp(m_old - m_tot)
                w_new = jnp.exp(m_new - m_tot)
                acc_tot = acc_old * w_old + acc_new * w_new
                l_tot = l_old * w_old + l_new * w_new
                if d == DILATIONS[-1]:
                    results.append((pieces, acc_tot / l_tot, None, None))
                else:
                    results.append((pieces, acc_tot, m_tot, l_tot))
            for (pieces, acc_val, m_val, l_val), u in zip(results, range(ATTN_GROUP)):
                for k, (slab, rows) in enumerate(pieces):
                    part = slice(k * qp, (k + 1) * qp)
                    if d == DILATIONS[-1]:
                        n = (g * ATTN_GROUP + u) % (seq // QBLK)
                        o_ref[0, pl.ds(n * QBLK + k, qp, stride=nslab), :] = acc_val[part]
                    else:
                        oacc[slab, rows, :] = acc_val[part]
                        macc[slab, rows, :] = m_val[part]
                        lacc[slab, rows, :] = l_val[part]

        ngroups = d * nblk // ATTN_GROUP
        stage_scores(0, 0)
        stage_scores(1, 1)
        stage_softmax(0)

        def steady(t, carry):
            g = 2 * t + 2
            stage_scores(g, 0)
            stage_softmax(1)
            stage_output(g - 2, 0)
            stage_scores(g + 1, 1)
            stage_softmax(0)
            stage_output(g - 1, 1)
            return carry

        lax.fori_loop(0, (ngroups - 2) // 2, steady, 0)
        stage_softmax(1)
        stage_output(ngroups - 2, 0)
        stage_output(ngroups - 1, 1)


def _attn_call(q, k, v):
    b, s, d = q.shape
    res, per = RESIDUES, s // RESIDUES
    spec = pl.BlockSpec((1, s, LANES), lambda bi, hp: (bi, 0, hp))
    return pl.pallas_call(
        functools.partial(_attn_kernel, seq=s),
        out_shape=jax.ShapeDtypeStruct(q.shape, F32),
        grid=(b, d // LANES),
        in_specs=[spec, spec, spec],
        out_specs=pl.BlockSpec((1, s, LANES), lambda bi, hp: (bi, 0, hp), pipeline_mode=pl.Buffered(1)),
        scratch_shapes=[
            pltpu.VMEM((res, per, LANES), F32),
            pltpu.VMEM((res, per, LANES), BF16),
            pltpu.VMEM((res, per, LANES), BF16),
            pltpu.VMEM((res, per, LANES), F32),
            pltpu.VMEM((res, per, LANES), F32),
            pltpu.VMEM((res, per, LANES), F32),
            pltpu.VMEM((3 * len(DILATIONS), QBLK, KWIN), F32),
            pltpu.VMEM((2, ATTN_GROUP, 2 * QBLK, KWIN), F32),
            pltpu.VMEM((2, ATTN_GROUP, 2 * QBLK, KWIN), BF16),
            pltpu.VMEM((2, ATTN_GROUP, QBLK, LANES), F32),
            pltpu.VMEM((2, ATTN_GROUP, QBLK, LANES), F32),
            pltpu.VMEM((QBLK * RESIDUES, LANES), F32),
        ],
        compiler_params=_params(("arbitrary", "arbitrary")),
        name="dilated_attn",
    )(q, k, v)


def _rope_tables(seq):
    pos = jnp.arange(seq, dtype=F32)
    inv_freq = ROPE_THETA ** (-jnp.arange(0, ROT_DIM, 2, dtype=F32) / ROT_DIM)
    ang = pos[:, None] * inv_freq[None, :]
    cos, sin = jnp.cos(ang), jnp.sin(ang)
    half = ROT_DIM // 2
    l64 = jnp.arange(LANES) % HEAD_DIM
    f = l64 % half
    cos_t = jnp.where(l64 < ROT_DIM, cos[:, f], 1.0)
    sin_t = jnp.where(l64 < half, -sin[:, f], jnp.where(l64 < ROT_DIM, sin[:, f], 0.0))
    return cos_t, sin_t


def kernel(x, mix_norm_g, mlp_norm_g, mlp_w1, mlp_w2, ab_w_in, a_spatial_w, a_spatial_b, a_vnorm_g, a_vnorm_b, b_conv_w, b_conv_b, b_norm_g, b_norm_b, ab_w_out, c_w_qkv, c_q_norm_g, c_k_norm_g, c_w_out):
    b, s, d = x.shape
    depth = mix_norm_g.shape[0]
    cos_t, sin_t = _rope_tables(s)
    w1s, w2s = mlp_w1.astype(BF16), mlp_w2.astype(BF16)
    wins, wouts = ab_w_in.astype(BF16), ab_w_out.astype(BF16)
    wqkvs, wos = c_w_qkv.astype(BF16), c_w_out.astype(BF16)
    for layer in range(depth):
        i = layer // 2
        g_mix = mix_norm_g[layer][None, :]
        g_mlp = mlp_norm_g[layer][None, :]
        if layer % 2 == 0:
            spw = a_spatial_w[i].astype(BF16).reshape(A_GROUPS // 2, 2 * CHUNK, CHUNK)
            spb = jnp.repeat(a_spatial_b[i].T, A_WIDTH // A_GROUPS, axis=1)
            x = _mixer_mlp_call(
                x.reshape(b * s, d), s, g_mix, wins, i, spw, spb,
                a_vnorm_g[i][None, :], a_vnorm_b[i][None, :], b_conv_w[i], b_conv_b[i][None, :],
                b_norm_g[i][None, :], b_norm_b[i][None, :], wouts, g_mlp, w1s, w2s, layer).reshape(b, s, d)
        else:
            qg = jnp.tile(c_q_norm_g[i], LANES // HEAD_DIM)[None, :]
            kg = jnp.tile(c_k_norm_g[i], LANES // HEAD_DIM)[None, :]
            q, k, v = _qkv_call(x.reshape(b * s, d), g_mix, wqkvs, i, qg, kg, cos_t, sin_t, s)
            a = _attn_call(q.reshape(b, s, d), k.reshape(b, s, d), v.reshape(b, s, d))
            x = _proj_mlp_call(x.reshape(b * s, d), a.reshape(b * s, d), wos, i, g_mlp, w1s, w2s,
                               layer).reshape(b, s, d)
    return x
```

```python
import functools

import jax
import jax.numpy as jnp
from jax import lax
from jax.experimental import pallas as pl
from jax.experimental.pallas import tpu as pltpu

F32 = jnp.float32
BF16 = jnp.bfloat16

D_MODEL = 1024
D_FF = 4 * D_MODEL
A_WIDTH = D_MODEL // 2
B_WIDTH = D_MODEL // 2
A_GROUPS = 8
CHUNK = 128
CONV_WIDTH = 31
HEAD_DIM = 64
ROT_DIM = HEAD_DIM // 4
ROPE_THETA = 500000.0
DILATIONS = (16, 4, 1)
HALF = 64
EPS = 1e-6
NEG = -1e30

LANES = 128
SUBLANES = 8
MXU_COLS = 256
HALO = 16
TM = 512
FF_CHUNK = 512
QBLK = 128
KWIN = 2 * QBLK
ATTN_GROUP = 2
RESIDUES = max(DILATIONS)
REGROUP_HOP = 4
VMEM_LIMIT = 56 * 1024 * 1024


def _params(sem, vmem=VMEM_LIMIT):
    return pltpu.CompilerParams(dimension_semantics=sem, vmem_limit_bytes=vmem)


def _rms(x, g):
    ms = jnp.mean(x * x, axis=-1, keepdims=True)
    return x * lax.rsqrt(ms + EPS) * g


def _layernorm(x, g, b):
    mu = jnp.mean(x, axis=-1, keepdims=True)
    xc = x - mu
    var = jnp.mean(xc * xc, axis=-1, keepdims=True)
    return xc * lax.rsqrt(var + EPS) * g + b


def _dot(a, b):
    return jnp.dot(a, b, preferred_element_type=F32)


def _run_interleaved(*parts):
    live = list(parts)
    while live:
        for part in list(live):
            if next(part, StopIteration) is StopIteration:
                live.remove(part)


def _layer_block(stack, index):
    zeros = (0,) * (stack.ndim - 1)
    return pl.BlockSpec((None,) + stack.shape[1:], lambda *_: (index,) + zeros, pipeline_mode=pl.Buffered(1))


def _mlp_steps(x, g, w1_ref, w2_ref):
    h = _rms(x, g).astype(BF16)
    acc = x
    for j in range(D_FF // FF_CHUNK):
        cols = slice(j * FF_CHUNK, (j + 1) * FF_CHUNK)
        a = _dot(h, w1_ref[:, cols])
        a = jnp.square(jnp.maximum(a, 0.0)).astype(BF16)
        acc = acc + _dot(a, w2_ref[cols, :])
        yield acc


def _mlp_body(x, g, w1_ref, w2_ref):
    for acc in _mlp_steps(x, g, w1_ref, w2_ref):
        pass
    return acc


def _proj_mlp_kernel(x_ref, a_ref, wo_ref, g_ref, w1_ref, w2_ref, o_ref):
    x = x_ref[...] + _dot(a_ref[...].astype(BF16), wo_ref[...])
    o_ref[...] = _mlp_body(x, g_ref[...], w1_ref, w2_ref)


def _proj_mlp_call(x2d, a2d, wo_stack, odd_index, g, w1_stack, w2_stack, layer):
    n = x2d.shape[0]
    const = lambda i: (0, 0)
    row = lambda i: (i, 0)
    return pl.pallas_call(
        _proj_mlp_kernel,
        out_shape=jax.ShapeDtypeStruct(x2d.shape, F32),
        grid=(n // TM,),
        in_specs=[
            pl.BlockSpec((TM, D_MODEL), row),
            pl.BlockSpec((TM, D_MODEL), row),
            _layer_block(wo_stack, odd_index),
            pl.BlockSpec((1, D_MODEL), const),
            _layer_block(w1_stack, layer),
            _layer_block(w2_stack, layer),
        ],
        out_specs=pl.BlockSpec((TM, D_MODEL), row),
        compiler_params=_params(("arbitrary",)),
        name="attn_out_proj_mlp",
    )(x2d, a2d, wo_stack, g, w1_stack, w2_stack)


def _mixer_mlp_kernel(xp_ref, x_ref, xn_ref, g_ref, win_ref, spw_ref, spb_ref, vg_ref, vb_ref,
                      cw_ref, cb_ref, cng_ref, cnb_ref, wout_ref, g2_ref, w1_ref, w2_ref,
                      o_ref, gs_ref, sh_ref, xm_ref, *, blocks_per_seq):
    t = pl.program_id(0)
    nsteps = pl.num_programs(0)

    def mlp():
        for acc in _mlp_steps(xm_ref[...], g2_ref[...], w1_ref, w2_ref):
            yield
        o_ref[...] = acc
        yield

    def mixer():
        i = t % blocks_per_seq
        last = blocks_per_seq - 1
        g = g_ref[...]
        x = x_ref[...]
        h = _rms(x, g).astype(BF16)
        h_all = jnp.concatenate(
            [_rms(xp_ref[...], g).astype(BF16), h, _rms(xn_ref[...], g).astype(BF16)], axis=0)

        zb = _dot(h_all, win_ref[:, 2 * A_WIDTH:])
        glu = zb[:, :B_WIDTH] * jax.nn.sigmoid(zb[:, B_WIDTH:])
        row = lax.broadcasted_iota(jnp.int32, glu.shape, 0)
        inside = ((row >= HALO) | (i > 0)) & ((row < TM + HALO) | (i < last))
        gs_ref[...] = jnp.where(inside, glu, 0.0)
        yield
        conv = jnp.broadcast_to(cb_ref[...], (TM, B_WIDTH))
        first_tap = HALO - CONV_WIDTH // 2
        for shift in range(SUBLANES):
            src = gs_ref
            if shift:
                sh_ref[...] = gs_ref[pl.ds(shift, TM + 2 * HALO - SUBLANES), :]
                src = sh_ref
            for base in range(0, 2 * HALO, SUBLANES):
                k = base + shift - first_tap
                if 0 <= k < CONV_WIDTH:
                    conv = conv + cw_ref[k:k + 1, :] * src[pl.ds(base, TM), :]
            if shift % 2 == 1 and shift < SUBLANES - 1:
                yield
        yb = _layernorm(conv, cng_ref[...], cnb_ref[...])
        yb = yb * jax.nn.sigmoid(yb)
        yield

        za = jax.nn.gelu(_dot(h, win_ref[:, :2 * A_WIDTH]))
        u = za[:, :A_WIDTH]
        v = _layernorm(za[:, A_WIDTH:], vg_ref[...], vb_ref[...]).astype(BF16)
        yield
        lane = lax.broadcasted_iota(jnp.int32, (CHUNK, LANES), 1)
        first_group = lane < LANES // 2
        ya_chunks = []
        for c in range(TM // CHUNK):
            vc = v[c * CHUNK:(c + 1) * CHUNK, :]
            parts = []
            for j in range(A_WIDTH // LANES):
                r = _dot(spw_ref[j], vc[:, j * LANES:(j + 1) * LANES])
                parts.append(jnp.where(first_group, r[:CHUNK], r[CHUNK:]))
            sv = jnp.concatenate(parts, axis=1) + spb_ref[...]
            ya_chunks.append(u[c * CHUNK:(c + 1) * CHUNK, :] * sv)
            if c % 2 == 1:
                yield
        ya = jnp.concatenate(ya_chunks, axis=0)

        out = _dot(ya.astype(BF16), wout_ref[:A_WIDTH, :]) + _dot(yb.astype(BF16), wout_ref[A_WIDTH:, :])
        xm_ref[...] = x + out
        yield

    @pl.when(t == 0)
    def _():
        _run_interleaved(mixer())

    @pl.when((t > 0) & (t < nsteps - 1))
    def _():
        _run_interleaved(mlp(), mixer())

    @pl.when(t == nsteps - 1)
    def _():
        _run_interleaved(mlp())


def _mixer_mlp_call(x2d, seq, g, win_stack, even_index, spw, spb, vg, vb, cw, cb, cng, cnb, wout_stack,
                    g2, w1_stack, w2_stack, layer):
    n, d = x2d.shape
    nblk = n // TM
    per = TM // HALO
    const2 = lambda t: (0, 0)
    const3 = lambda t: (0, 0, 0)
    cur = lambda t: jnp.minimum(t, nblk - 1)
    resident = dict(pipeline_mode=pl.Buffered(1))
    return pl.pallas_call(
        functools.partial(_mixer_mlp_kernel, blocks_per_seq=seq // TM),
        out_shape=jax.ShapeDtypeStruct(x2d.shape, F32),
        grid=(nblk + 1,),
        in_specs=[
            pl.BlockSpec((HALO, d), lambda t: (jnp.maximum(cur(t) * per - 1, 0), 0)),
            pl.BlockSpec((TM, d), lambda t: (cur(t), 0)),
            pl.BlockSpec((HALO, d), lambda t: (jnp.minimum((cur(t) + 1) * per, n // HALO - 1), 0)),
            pl.BlockSpec((1, d), const2),
            _layer_block(win_stack, even_index),
            pl.BlockSpec(spw.shape, const3, **resident),
            pl.BlockSpec(spb.shape, const2, **resident),
            pl.BlockSpec((1, A_WIDTH), const2),
            pl.BlockSpec((1, A_WIDTH), const2),
            pl.BlockSpec(cw.shape, const2),
            pl.BlockSpec((1, B_WIDTH), const2),
            pl.BlockSpec((1, B_WIDTH), const2),
            pl.BlockSpec((1, B_WIDTH), const2),
            _layer_block(wout_stack, even_index),
            pl.BlockSpec((1, d), const2),
            _layer_block(w1_stack, layer),
            _layer_block(w2_stack, layer),
        ],
        out_specs=pl.BlockSpec((TM, d), lambda t: (jnp.maximum(t - 1, 0), 0)),
        scratch_shapes=[pltpu.VMEM((TM + 2 * HALO, B_WIDTH), F32),
                        pltpu.VMEM((TM + 2 * HALO - SUBLANES, B_WIDTH), F32),
                        pltpu.VMEM((TM, D_MODEL), F32)],
        compiler_params=_params(("arbitrary",)),
        name="mixer_ab_mlp",
    )(x2d, x2d, x2d, g, win_stack, spw, spb, vg, vb, cw, cb, cng, cnb, wout_stack, g2, w1_stack, w2_stack)


def _qkv_kernel(x_ref, g_ref, w_ref, qg_ref, kg_ref, cos_ref, sin_ref, q_ref, k_ref, v_ref, raw_ref):
    i = pl.program_id(0)
    nsteps = pl.num_programs(0)

    def project():
        h = _rms(x_ref[...], g_ref[...]).astype(BF16)
        raw_ref[...] = _dot(h, w_ref[...])

    def finish():
        lane = lax.broadcasted_iota(jnp.int32, (TM, LANES), 1)
        low = (lane % HEAD_DIM) < ROT_DIM // 2
        cos = cos_ref[...]
        sin = sin_ref[...]
        hr = lax.broadcasted_iota(jnp.int32, (MXU_COLS, MXU_COLS), 0) // HEAD_DIM
        hc = lax.broadcasted_iota(jnp.int32, (MXU_COLS, MXU_COLS), 1) // HEAD_DIM
        same_head = jnp.where(hr == hc, 1.0, 0.0).astype(BF16)

        def head_norm_rope(t4, gain):
            ms = _dot((t4 * t4).astype(BF16), same_head) * (1.0 / HEAD_DIM)
            outs = []
            for half in range(MXU_COLS // LANES):
                sl = slice(half * LANES, (half + 1) * LANES)
                t = t4[:, sl] * lax.rsqrt(ms[:, sl] + EPS) * gain
                partner = jnp.where(low, pltpu.roll(t, LANES - ROT_DIM // 2, 1), pltpu.roll(t, ROT_DIM // 2, 1))
                outs.append(t * cos + partner * sin)
            return jnp.concatenate(outs, axis=1)

        for j in range(D_MODEL // MXU_COLS):
            cols = slice(j * MXU_COLS, (j + 1) * MXU_COLS)
            q_ref[:, cols] = head_norm_rope(raw_ref[:, cols], qg_ref[...]) * (HEAD_DIM ** -0.5)
            kcols = slice(D_MODEL + j * MXU_COLS, D_MODEL + (j + 1) * MXU_COLS)
            k_ref[:, cols] = head_norm_rope(raw_ref[:, kcols], kg_ref[...])
        v_ref[...] = raw_ref[:, 2 * D_MODEL:]

    @pl.when(i == 0)
    def _():
        project()

    @pl.when((i > 0) & (i < nsteps - 1))
    def _():
        finish()
        project()

    @pl.when(i == nsteps - 1)
    def _():
        finish()


def _qkv_call(x2d, g, w_stack, odd_index, qg, kg, cos_t, sin_t, seq):
    n = x2d.shape[0]
    nblk = n // TM
    pos_blocks = seq // TM
    const = lambda i: (0, 0)
    lagged = lambda i: (jnp.maximum(i - 1, 0), 0)
    lagged_pos = lambda i: (jnp.maximum(i - 1, 0) % pos_blocks, 0)
    out = jax.ShapeDtypeStruct(x2d.shape, F32)
    return pl.pallas_call(
        _qkv_kernel,
        out_shape=(out, out, out),
        grid=(nblk + 1,),
        in_specs=[
            pl.BlockSpec((TM, D_MODEL), lambda i: (jnp.minimum(i, nblk - 1), 0)),
            pl.BlockSpec((1, D_MODEL), const),
            _layer_block(w_stack, odd_index),
            pl.BlockSpec((1, LANES), const),
            pl.BlockSpec((1, LANES), const),
            pl.BlockSpec((TM, LANES), lagged_pos),
            pl.BlockSpec((TM, LANES), lagged_pos),
        ],
        out_specs=(pl.BlockSpec((TM, D_MODEL), lagged),) * 3,
        scratch_shapes=[pltpu.VMEM((TM, 3 * D_MODEL), F32)],
        compiler_params=_params(("arbitrary",)),
        name="qkv",
    )(x2d, g, w_stack, qg, kg, cos_t, sin_t)


def _attn_kernel(qn_ref, kn_ref, vn_ref, o_ref, q_ref, k_ref, v_ref, oacc, macc, lacc, bias_ref,
                 s_scr, p_scr, m_scr, stage, *, seq):
    per = seq // RESIDUES

    def regroup(c, carry):
        hop = REGROUP_HOP
        span = QBLK * RESIDUES
        seg = span // hop
        for src_ref, dst_ref in ((qn_ref, q_ref), (kn_ref, k_ref), (vn_ref, v_ref)):
            for r1 in range(hop):
                for part in range(seg // QBLK):
                    rows = pl.ds(c * span + r1 + part * QBLK * hop, QBLK, stride=hop)
                    stage[pl.ds(r1 * seg + part * QBLK, QBLK), :] = src_ref[0, rows, :]
            for r in range(RESIDUES):
                r1, r2 = r % hop, r // hop
                val = stage[pl.ds(r1 * seg + r2, QBLK, stride=RESIDUES // hop), :]
                dst_ref[r, pl.ds(pl.multiple_of(c * QBLK, QBLK), QBLK), :] = val.astype(dst_ref.dtype)
        return carry

    lax.fori_loop(0, per // QBLK, regroup, 0)
    lane = lax.broadcasted_iota(jnp.int32, (QBLK, LANES), 1)
    head0 = lane < HEAD_DIM
    ones_keys = jnp.ones((KWIN, LANES), BF16)
    row = lax.broadcasted_iota(jnp.int32, (QBLK, KWIN), 0)
    col = lax.broadcasted_iota(jnp.int32, (QBLK, KWIN), 1)
    for pat, d in enumerate(DILATIONS):
        nslab = RESIDUES // d
        qp, kp = QBLK // nslab, KWIN // nslab
        qpos = nslab * (row % qp) + row // qp
        kpos = col if d == 1 else nslab * (col % kp) + col // kp
        for which, shift in enumerate((HALF, 0, 2 * HALF)):
            diff = kpos - qpos - shift
            bias_ref[3 * pat + which] = jnp.where((diff >= -HALF) & (diff <= HALF), 0.0, NEG)

    for pat, d in enumerate(DILATIONS):
        nslab = RESIDUES // d
        qp, kp = QBLK // nslab, KWIN // nslab
        nblk = seq // d // QBLK

        def locate(idx, nblk=nblk):
            return idx // nblk, idx % nblk

        def slab_rows(seg, start, size, d=d, nslab=nslab):
            return [(seg + d * j, pl.ds(start, size)) for j in range(nslab)]

        def gather(ref, pieces):
            parts = [ref[slab, rows, :] for slab, rows in pieces]
            return parts[0] if len(parts) == 1 else jnp.concatenate(parts, axis=0)

        def query_pieces(seg, n, qp=qp):
            return slab_rows(seg, pl.multiple_of(n * qp, qp), qp)

        def key_block(ref, ref_n, seg, n, d=d, qp=qp, kp=kp):
            if d == 1:
                start = jnp.clip(n * QBLK - HALF, 0, seq - KWIN)
                return ref_n[0, pl.ds(pl.multiple_of(start, HALF), KWIN), :].astype(BF16)
            start = jnp.clip(n * qp - kp // 4, 0, per - kp)
            return gather(ref, slab_rows(seg, pl.multiple_of(start, kp // 4), kp))

        def stage_scores(g, slot, pat=pat, nblk=nblk):
            for u in range(ATTN_GROUP):
                seg, n = locate(g * ATTN_GROUP + u)
                qb = gather(q_ref, query_pieces(seg, n)).astype(BF16)
                kb = key_block(k_ref, kn_ref, seg, n)
                bias = bias_ref[3 * pat + jnp.where(n == 0, 1, jnp.where(n == nblk - 1, 2, 0))]
                zero = jnp.zeros_like(qb)
                qq = jnp.concatenate([jnp.where(head0, qb, zero), jnp.where(head0, zero, qb)], axis=0)
                s = lax.dot_general(qq, kb, (((1,), (1,)), ((), ())), preferred_element_type=F32)
                s_scr[slot, u, :QBLK, :] = s[:QBLK] + bias
                s_scr[slot, u, QBLK:, :] = s[QBLK:] + bias

        def stage_softmax(slot):
            for u in range(ATTN_GROUP):
                s = s_scr[slot, u]
                m = jnp.max(s, axis=-1, keepdims=True)
                p_scr[slot, u] = jnp.exp(s - m).astype(BF16)
                m_scr[slot, u] = jnp.where(head0, m[:QBLK], m[QBLK:])

        def stage_output(g, slot, d=d, qp=qp, nslab=nslab):
            results = []
            for u in range(ATTN_GROUP):
                seg, n = locate(g * ATTN_GROUP + u)
                v_ones = jnp.concatenate([key_block(v_ref, vn_ref, seg, n), ones_keys], axis=1)
                pv = _dot(p_scr[slot, u], v_ones)
                acc_new = jnp.where(head0, pv[:QBLK, :LANES], pv[QBLK:, :LANES])
                l_new = jnp.where(head0, pv[:QBLK, LANES:], pv[QBLK:, LANES:])
                m_new = m_scr[slot, u]
                pieces = query_pieces(seg, n)
                if d == DILATIONS[0]:
                    results.append((pieces, acc_new, m_new, l_new))
                    continue
                m_old = jnp.concatenate([macc[slab, rows, :] for slab, rows in pieces], axis=0)
                acc_old = jnp.concatenate([oacc[slab, rows, :] for slab, rows in pieces], axis=0)
                l_old = jnp.concatenate([lacc[slab, rows, :] for slab, rows in pieces], axis=0)
                m_tot = jnp.maximum(m_old, m_new)
                w_old = jnp.exp(m_old - m_tot)
                w_new = jnp.exp(m_new - m_tot)
                acc_tot = acc_old * w_old + acc_new * w_new
                l_tot = l_old * w_old + l_new * w_new
                if d == DILATIONS[-1]:
                    results.append((pieces, acc_tot / l_tot, None, None))
                else:
                    results.append((pieces, acc_tot, m_tot, l_tot))
            for (pieces, acc_val, m_val, l_val), u in zip(results, range(ATTN_GROUP)):
                for k, (slab, rows) in enumerate(pieces):
                    part = slice(k * qp, (k + 1) * qp)
                    if d == DILATIONS[-1]:
                        n = (g * ATTN_GROUP + u) % (seq // QBLK)
                        o_ref[0, pl.ds(n * QBLK + k, qp, stride=nslab), :] = acc_val[part]
                    else:
                        oacc[slab, rows, :] = acc_val[part]
                        macc[slab, rows, :] = m_val[part]
                        lacc[slab, rows, :] = l_val[part]

        ngroups = d * nblk // ATTN_GROUP
        stage_scores(0, 0)
        stage_scores(1, 1)
        stage_softmax(0)

        def steady(t, carry):
            g = 2 * t + 2
            stage_scores(g, 0)
            stage_softmax(1)
            stage_output(g - 2, 0)
            stage_scores(g + 1, 1)
            stage_softmax(0)
            stage_output(g - 1, 1)
            return carry

        lax.fori_loop(0, (ngroups - 2) // 2, steady, 0)
        stage_softmax(1)
        stage_output(ngroups - 2, 0)
        stage_output(ngroups - 1, 1)


def _attn_call(q, k, v):
    b, s, d = q.shape
    res, per = RESIDUES, s // RESIDUES
    spec = pl.BlockSpec((1, s, LANES), lambda bi, hp: (bi, 0, hp))
    return pl.pallas_call(
        functools.partial(_attn_kernel, seq=s),
        out_shape=jax.ShapeDtypeStruct(q.shape, F32),
        grid=(b, d // LANES),
        in_specs=[spec, spec, spec],
        out_specs=pl.BlockSpec((1, s, LANES), lambda bi, hp: (bi, 0, hp), pipeline_mode=pl.Buffered(1)),
        scratch_shapes=[
            pltpu.VMEM((res, per, LANES), F32),
            pltpu.VMEM((res, per, LANES), BF16),
            pltpu.VMEM((res, per, LANES), BF16),
            pltpu.VMEM((res, per, LANES), F32),
            pltpu.VMEM((res, per, LANES), F32),
            pltpu.VMEM((res, per, LANES), F32),
            pltpu.VMEM((3 * len(DILATIONS), QBLK, KWIN), F32),
            pltpu.VMEM((2, ATTN_GROUP, 2 * QBLK, KWIN), F32),
            pltpu.VMEM((2, ATTN_GROUP, 2 * QBLK, KWIN), BF16),
            pltpu.VMEM((2, ATTN_GROUP, QBLK, LANES), F32),
            pltpu.VMEM((QBLK * RESIDUES, LANES), F32),
        ],
        compiler_params=_params(("arbitrary", "arbitrary")),
        name="dilated_attn",
    )(q, k, v)


def _rope_tables(seq):
    pos = jnp.arange(seq, dtype=F32)
    inv_freq = ROPE_THETA ** (-jnp.arange(0, ROT_DIM, 2, dtype=F32) / ROT_DIM)
    ang = pos[:, None] * inv_freq[None, :]
    cos, sin = jnp.cos(ang), jnp.sin(ang)
    half = ROT_DIM // 2
    l64 = jnp.arange(LANES) % HEAD_DIM
    f = l64 % half
    cos_t = jnp.where(l64 < ROT_DIM, cos[:, f], 1.0)
    sin_t = jnp.where(l64 < half, -sin[:, f], jnp.where(l64 < ROT_DIM, sin[:, f], 0.0))
    return cos_t, sin_t


def kernel(x, mix_norm_g, mlp_norm_g, mlp_w1, mlp_w2, ab_w_in, a_spatial_w, a_spatial_b, a_vnorm_g, a_vnorm_b, b_conv_w, b_conv_b, b_norm_g, b_norm_b, ab_w_out, c_w_qkv, c_q_norm_g, c_k_norm_g, c_w_out):
    b, s, d = x.shape
    depth = mix_norm_g.shape[0]
    cos_t, sin_t = _rope_tables(s)
    w1s, w2s = mlp_w1.astype(BF16), mlp_w2.astype(BF16)
    wins, wouts = ab_w_in.astype(BF16), ab_w_out.astype(BF16)
    wqkvs, wos = c_w_qkv.astype(BF16), c_w_out.astype(BF16)
    for layer in range(depth):
        i = layer // 2
        g_mix = mix_norm_g[layer][None, :]
        g_mlp = mlp_norm_g[layer][None, :]
        if layer % 2 == 0:
            spw = a_spatial_w[i].astype(BF16).reshape(A_GROUPS // 2, 2 * CHUNK, CHUNK)
            spb = jnp.repeat(a_spatial_b[i].T, A_WIDTH // A_GROUPS, axis=1)
            x = _mixer_mlp_call(
                x.reshape(b * s, d), s, g_mix, wins, i, spw, spb,
                a_vnorm_g[i][None, :], a_vnorm_b[i][None, :], b_conv_w[i], b_conv_b[i][None, :],
                b_norm_g[i][None, :], b_norm_b[i][None, :], wouts, g_mlp, w1s, w2s, layer).reshape(b, s, d)
        else:
            qg = jnp.tile(c_q_norm_g[i], LANES // HEAD_DIM)[None, :]
            kg = jnp.tile(c_k_norm_g[i], LANES // HEAD_DIM)[None, :]
            q, k, v = _qkv_call(x.reshape(b * s, d), g_mix, wqkvs, i, qg, kg, cos_t, sin_t, s)
            a = _attn_call(q.reshape(b, s, d), k.reshape(b, s, d), v.reshape(b, s, d))
            x = _proj_mlp_call(x.reshape(b * s, d), a.reshape(b * s, d), wos, i, g_mlp, w1s, w2s,
                               layer).reshape(b, s, d)
    return x
```

```python
import functools

import jax
import jax.numpy as jnp
from jax import lax
from jax.experimental import pallas as pl
from jax.experimental.pallas import tpu as pltpu

F32 = jnp.float32
BF16 = jnp.bfloat16

D_MODEL = 1024
D_FF = 4 * D_MODEL
A_WIDTH = D_MODEL // 2
B_WIDTH = D_MODEL // 2
A_GROUPS = 8
CHUNK = 128
CONV_WIDTH = 31
HEAD_DIM = 64
ROT_DIM = HEAD_DIM // 4
ROPE_THETA = 500000.0
DILATIONS = (16, 4, 1)
HALF = 64
EPS = 1e-6
NEG = -1e30

LANES = 128
SUBLANES = 8
MXU_COLS = 256
HALO = 16
TM = 512
FF_CHUNK = 512
QBLK = 128
KWIN = 2 * QBLK
ATTN_GROUP = 2
ATTN_UNROLL = 3
RESIDUES = max(DILATIONS)
REGROUP_HOP = 4
VMEM_LIMIT = 56 * 1024 * 1024


def _params(sem, vmem=VMEM_LIMIT):
    return pltpu.CompilerParams(dimension_semantics=sem, vmem_limit_bytes=vmem)


def _rms(x, g):
    ms = jnp.mean(x * x, axis=-1, keepdims=True)
    return x * lax.rsqrt(ms + EPS) * g


def _layernorm(x, g, b):
    mu = jnp.mean(x, axis=-1, keepdims=True)
    xc = x - mu
    var = jnp.mean(xc * xc, axis=-1, keepdims=True)
    return xc * lax.rsqrt(var + EPS) * g + b


def _dot(a, b):
    return jnp.dot(a, b, preferred_element_type=F32)


def _run_interleaved(*parts):
    live = list(parts)
    while live:
        for part in list(live):
            if next(part, StopIteration) is StopIteration:
                live.remove(part)


def _layer_block(stack, index):
    zeros = (0,) * (stack.ndim - 1)
    return pl.BlockSpec((None,) + stack.shape[1:], lambda *_: (index,) + zeros, pipeline_mode=pl.Buffered(1))


def _mlp_steps(x, g, w1_ref, w2_ref):
    h = _rms(x, g).astype(BF16)
    acc = x
    for j in range(D_FF // FF_CHUNK):
        cols = slice(j * FF_CHUNK, (j + 1) * FF_CHUNK)
        a = _dot(h, w1_ref[:, cols])
        a = jnp.square(jnp.maximum(a, 0.0)).astype(BF16)
        acc = acc + _dot(a, w2_ref[cols, :])
        yield acc


def _mlp_body(x, g, w1_ref, w2_ref):
    for acc in _mlp_steps(x, g, w1_ref, w2_ref):
        pass
    return acc


def _proj_mlp_kernel(x_ref, a_ref, wo_ref, g_ref, w1_ref, w2_ref, o_ref):
    x = x_ref[...] + _dot(a_ref[...].astype(BF16), wo_ref[...])
    o_ref[...] = _mlp_body(x, g_ref[...], w1_ref, w2_ref)


def _proj_mlp_call(x2d, a2d, wo_stack, odd_index, g, w1_stack, w2_stack, layer):
    n = x2d.shape[0]
    const = lambda i: (0, 0)
    row = lambda i: (i, 0)
    return pl.pallas_call(
        _proj_mlp_kernel,
        out_shape=jax.ShapeDtypeStruct(x2d.shape, F32),
        grid=(n // TM,),
        in_specs=[
            pl.BlockSpec((TM, D_MODEL), row),
            pl.BlockSpec((TM, D_MODEL), row),
            _layer_block(wo_stack, odd_index),
            pl.BlockSpec((1, D_MODEL), const),
            _layer_block(w1_stack, layer),
            _layer_block(w2_stack, layer),
        ],
        out_specs=pl.BlockSpec((TM, D_MODEL), row),
        compiler_params=_params(("arbitrary",)),
        name="attn_out_proj_mlp",
    )(x2d, a2d, wo_stack, g, w1_stack, w2_stack)


def _mixer_mlp_kernel(xp_ref, x_ref, xn_ref, g_ref, win_ref, spw_ref, spb_ref, vg_ref, vb_ref,
                      cw_ref, cb_ref, cng_ref, cnb_ref, wout_ref, g2_ref, w1_ref, w2_ref,
                      o_ref, gs_ref, sh_ref, xm_ref, *, blocks_per_seq):
    t = pl.program_id(0)
    nsteps = pl.num_programs(0)

    def mlp():
        for acc in _mlp_steps(xm_ref[...], g2_ref[...], w1_ref, w2_ref):
            yield
        o_ref[...] = acc
        yield

    def mixer():
        i = t % blocks_per_seq
        last = blocks_per_seq - 1
        g = g_ref[...]
        x = x_ref[...]
        h = _rms(x, g).astype(BF16)
        h_all = jnp.concatenate(
            [_rms(xp_ref[...], g).astype(BF16), h, _rms(xn_ref[...], g).astype(BF16)], axis=0)

        zb = _dot(h_all, win_ref[:, 2 * A_WIDTH:])
        glu = zb[:, :B_WIDTH] * jax.nn.sigmoid(zb[:, B_WIDTH:])
        row = lax.broadcasted_iota(jnp.int32, glu.shape, 0)
        inside = ((row >= HALO) | (i > 0)) & ((row < TM + HALO) | (i < last))
        gs_ref[...] = jnp.where(inside, glu, 0.0)
        yield
        conv = jnp.broadcast_to(cb_ref[...], (TM, B_WIDTH))
        first_tap = HALO - CONV_WIDTH // 2
        for shift in range(SUBLANES):
            src = gs_ref
            if shift:
                sh_ref[...] = gs_ref[pl.ds(shift, TM + 2 * HALO - SUBLANES), :]
                src = sh_ref
            for base in range(0, 2 * HALO, SUBLANES):
                k = base + shift - first_tap
                if 0 <= k < CONV_WIDTH:
                    conv = conv + cw_ref[k:k + 1, :] * src[pl.ds(base, TM), :]
            if shift % 2 == 1 and shift < SUBLANES - 1:
                yield
        yb = _layernorm(conv, cng_ref[...], cnb_ref[...])
        yb = yb * jax.nn.sigmoid(yb)
        yield

        za = jax.nn.gelu(_dot(h, win_ref[:, :2 * A_WIDTH]))
        u = za[:, :A_WIDTH]
        v = _layernorm(za[:, A_WIDTH:], vg_ref[...], vb_ref[...]).astype(BF16)
        yield
        lane = lax.broadcasted_iota(jnp.int32, (CHUNK, LANES), 1)
        first_group = lane < LANES // 2
        ya_chunks = []
        for c in range(TM // CHUNK):
            vc = v[c * CHUNK:(c + 1) * CHUNK, :]
            parts = []
            for j in range(A_WIDTH // LANES):
                r = _dot(spw_ref[j], vc[:, j * LANES:(j + 1) * LANES])
                parts.append(jnp.where(first_group, r[:CHUNK], r[CHUNK:]))
            sv = jnp.concatenate(parts, axis=1) + spb_ref[...]
            ya_chunks.append(u[c * CHUNK:(c + 1) * CHUNK, :] * sv)
            if c % 2 == 1:
                yield
        ya = jnp.concatenate(ya_chunks, axis=0)

        out = _dot(ya.astype(BF16), wout_ref[:A_WIDTH, :]) + _dot(yb.astype(BF16), wout_ref[A_WIDTH:, :])
        xm_ref[...] = x + out
        yield

    @pl.when(t == 0)
    def _():
        _run_interleaved(mixer())

    @pl.when((t > 0) & (t < nsteps - 1))
    def _():
        _run_interleaved(mlp(), mixer())

    @pl.when(t == nsteps - 1)
    def _():
        _run_interleaved(mlp())


def _mixer_mlp_call(x2d, seq, g, win_stack, even_index, spw, spb, vg, vb, cw, cb, cng, cnb, wout_stack,
                    g2, w1_stack, w2_stack, layer):
    n, d = x2d.shape
    nblk = n // TM
    per = TM // HALO
    const2 = lambda t: (0, 0)
    const3 = lambda t: (0, 0, 0)
    cur = lambda t: jnp.minimum(t, nblk - 1)
    resident = dict(pipeline_mode=pl.Buffered(1))
    return pl.pallas_call(
        functools.partial(_mixer_mlp_kernel, blocks_per_seq=seq // TM),
        out_shape=jax.ShapeDtypeStruct(x2d.shape, F32),
        grid=(nblk + 1,),
        in_specs=[
            pl.BlockSpec((HALO, d), lambda t: (jnp.maximum(cur(t) * per - 1, 0), 0)),
            pl.BlockSpec((TM, d), lambda t: (cur(t), 0)),
            pl.BlockSpec((HALO, d), lambda t: (jnp.minimum((cur(t) + 1) * per, n // HALO - 1), 0)),
            pl.BlockSpec((1, d), const2),
            _layer_block(win_stack, even_index),
            pl.BlockSpec(spw.shape, const3, **resident),
            pl.BlockSpec(spb.shape, const2, **resident),
            pl.BlockSpec((1, A_WIDTH), const2),
            pl.BlockSpec((1, A_WIDTH), const2),
            pl.BlockSpec(cw.shape, const2),
            pl.BlockSpec((1, B_WIDTH), const2),
            pl.BlockSpec((1, B_WIDTH), const2),
            pl.BlockSpec((1, B_WIDTH), const2),
            _layer_block(wout_stack, even_index),
            pl.BlockSpec((1, d), const2),
            _layer_block(w1_stack, layer),
            _layer_block(w2_stack, layer),
        ],
        out_specs=pl.BlockSpec((TM, d), lambda t: (jnp.maximum(t - 1, 0), 0)),
        scratch_shapes=[pltpu.VMEM((TM + 2 * HALO, B_WIDTH), F32),
                        pltpu.VMEM((TM + 2 * HALO - SUBLANES, B_WIDTH), F32),
                        pltpu.VMEM((TM, D_MODEL), F32)],
        compiler_params=_params(("arbitrary",)),
        name="mixer_ab_mlp",
    )(x2d, x2d, x2d, g, win_stack, spw, spb, vg, vb, cw, cb, cng, cnb, wout_stack, g2, w1_stack, w2_stack)


def _qkv_kernel(x_ref, g_ref, w_ref, qg_ref, kg_ref, cos_ref, sin_ref, q_ref, k_ref, v_ref, raw_ref):
    i = pl.program_id(0)
    nsteps = pl.num_programs(0)

    def project():
        h = _rms(x_ref[...], g_ref[...]).astype(BF16)
        raw_ref[...] = _dot(h, w_ref[...])

    def finish():
        lane = lax.broadcasted_iota(jnp.int32, (TM, LANES), 1)
        low = (lane % HEAD_DIM) < ROT_DIM // 2
        cos = cos_ref[...]
        sin = sin_ref[...]
        hr = lax.broadcasted_iota(jnp.int32, (MXU_COLS, MXU_COLS), 0) // HEAD_DIM
        hc = lax.broadcasted_iota(jnp.int32, (MXU_COLS, MXU_COLS), 1) // HEAD_DIM
        same_head = jnp.where(hr == hc, 1.0, 0.0).astype(BF16)

        def head_norm_rope(t4, gain):
            ms = _dot((t4 * t4).astype(BF16), same_head) * (1.0 / HEAD_DIM)
            outs = []
            for half in range(MXU_COLS // LANES):
                sl = slice(half * LANES, (half + 1) * LANES)
                t = t4[:, sl] * lax.rsqrt(ms[:, sl] + EPS) * gain
                partner = jnp.where(low, pltpu.roll(t, LANES - ROT_DIM // 2, 1), pltpu.roll(t, ROT_DIM // 2, 1))
                outs.append(t * cos + partner * sin)
            return jnp.concatenate(outs, axis=1)

        for j in range(D_MODEL // MXU_COLS):
            cols = slice(j * MXU_COLS, (j + 1) * MXU_COLS)
            q_ref[:, cols] = head_norm_rope(raw_ref[:, cols], qg_ref[...]) * (HEAD_DIM ** -0.5)
            kcols = slice(D_MODEL + j * MXU_COLS, D_MODEL + (j + 1) * MXU_COLS)
            k_ref[:, cols] = head_norm_rope(raw_ref[:, kcols], kg_ref[...])
        v_ref[...] = raw_ref[:, 2 * D_MODEL:]

    @pl.when(i == 0)
    def _():
        project()

    @pl.when((i > 0) & (i < nsteps - 1))
    def _():
        finish()
        project()

    @pl.when(i == nsteps - 1)
    def _():
        finish()


def _qkv_call(x2d, g, w_stack, odd_index, qg, kg, cos_t, sin_t, seq):
    n = x2d.shape[0]
    nblk = n // TM
    pos_blocks = seq // TM
    const = lambda i: (0, 0)
    lagged = lambda i: (jnp.maximum(i - 1, 0), 0)
    lagged_pos = lambda i: (jnp.maximum(i - 1, 0) % pos_blocks, 0)
    out = jax.ShapeDtypeStruct(x2d.shape, F32)
    return pl.pallas_call(
        _qkv_kernel,
        out_shape=(out, out, out),
        grid=(nblk + 1,),
        in_specs=[
            pl.BlockSpec((TM, D_MODEL), lambda i: (jnp.minimum(i, nblk - 1), 0)),
            pl.BlockSpec((1, D_MODEL), const),
            _layer_block(w_stack, odd_index),
            pl.BlockSpec((1, LANES), const),
            pl.BlockSpec((1, LANES), const),
            pl.BlockSpec((TM, LANES), lagged_pos),
            pl.BlockSpec((TM, LANES), lagged_pos),
        ],
        out_specs=(pl.BlockSpec((TM, D_MODEL), lagged),) * 3,
        scratch_shapes=[pltpu.VMEM((TM, 3 * D_MODEL), F32)],
        compiler_params=_params(("arbitrary",)),
        name="qkv",
    )(x2d, g, w_stack, qg, kg, cos_t, sin_t)


def _attn_kernel(qn_ref, kn_ref, vn_ref, o_ref, q_ref, k_ref, v_ref, oacc, macc, lacc, bias_ref,
                 s_scr, p_scr, m_scr, stage, *, seq):
    per = seq // RESIDUES

    def regroup(c, carry):
        hop = REGROUP_HOP
        span = QBLK * RESIDUES
        seg = span // hop
        for src_ref, dst_ref in ((qn_ref, q_ref), (kn_ref, k_ref), (vn_ref, v_ref)):
            for r1 in range(hop):
                for part in range(seg // QBLK):
                    rows = pl.ds(c * span + r1 + part * QBLK * hop, QBLK, stride=hop)
                    stage[pl.ds(r1 * seg + part * QBLK, QBLK), :] = src_ref[0, rows, :]
            for r in range(RESIDUES):
                r1, r2 = r % hop, r // hop
                val = stage[pl.ds(r1 * seg + r2, QBLK, stride=RESIDUES // hop), :]
                dst_ref[r, pl.ds(pl.multiple_of(c * QBLK, QBLK), QBLK), :] = val.astype(dst_ref.dtype)
        return carry

    lax.fori_loop(0, per // QBLK, regroup, 0)
    lane = lax.broadcasted_iota(jnp.int32, (QBLK, LANES), 1)
    head0 = lane < HEAD_DIM
    ones_keys = jnp.ones((KWIN, LANES), BF16)
    row = lax.broadcasted_iota(jnp.int32, (QBLK, KWIN), 0)
    col = lax.broadcasted_iota(jnp.int32, (QBLK, KWIN), 1)
    for pat, d in enumerate(DILATIONS):
        nslab = RESIDUES // d
        qp, kp = QBLK // nslab, KWIN // nslab
        qpos = nslab * (row % qp) + row // qp
        kpos = col if d == 1 else nslab * (col % kp) + col // kp
        for which, shift in enumerate((HALF, 0, 2 * HALF)):
            diff = kpos - qpos - shift
            bias_ref[3 * pat + which] = jnp.where((diff >= -HALF) & (diff <= HALF), 0.0, NEG)

    for pat, d in enumerate(DILATIONS):
        nslab = RESIDUES // d
        qp, kp = QBLK // nslab, KWIN // nslab
        nblk = seq // d // QBLK

        def locate(idx, nblk=nblk):
            return idx // nblk, idx % nblk

        def slab_rows(seg, start, size, d=d, nslab=nslab):
            return [(seg + d * j, pl.ds(start, size)) for j in range(nslab)]

        def gather(ref, pieces):
            parts = [ref[slab, rows, :] for slab, rows in pieces]
            return parts[0] if len(parts) == 1 else jnp.concatenate(parts, axis=0)

        def query_pieces(seg, n, qp=qp):
            return slab_rows(seg, pl.multiple_of(n * qp, qp), qp)

        def key_block(ref, ref_n, seg, n, d=d, qp=qp, kp=kp):
            if d == 1:
                start = jnp.clip(n * QBLK - HALF, 0, seq - KWIN)
                return ref_n[0, pl.ds(pl.multiple_of(start, HALF), KWIN), :].astype(BF16)
            start = jnp.clip(n * qp - kp // 4, 0, per - kp)
            return gather(ref, slab_rows(seg, pl.multiple_of(start, kp // 4), kp))

        def stage_scores(g, slot, pat=pat, nblk=nblk):
            for u in range(ATTN_GROUP):
                seg, n = locate(g * ATTN_GROUP + u)
                qb = gather(q_ref, query_pieces(seg, n)).astype(BF16)
                kb = key_block(k_ref, kn_ref, seg, n)
                bias = bias_ref[3 * pat + jnp.where(n == 0, 1, jnp.where(n == nblk - 1, 2, 0))]
                zero = jnp.zeros_like(qb)
                qq = jnp.concatenate([jnp.where(head0, qb, zero), jnp.where(head0, zero, qb)], axis=0)
                s = lax.dot_general(qq, kb, (((1,), (1,)), ((), ())), preferred_element_type=F32)
                s_scr[slot, u, :QBLK, :] = s[:QBLK] + bias
                s_scr[slot, u, QBLK:, :] = s[QBLK:] + bias

        def stage_softmax(slot):
            for u in range(ATTN_GROUP):
                s = s_scr[slot, u]
                m = jnp.max(s, axis=-1, keepdims=True)
                p_scr[slot, u] = jnp.exp(s - m).astype(BF16)
                m_scr[slot, u] = jnp.where(head0, m[:QBLK], m[QBLK:])

        def stage_output(g, slot, d=d, qp=qp, nslab=nslab):
            results = []
            for u in range(ATTN_GROUP):
                seg, n = locate(g * ATTN_GROUP + u)
                v_ones = jnp.concatenate([key_block(v_ref, vn_ref, seg, n), ones_keys], axis=1)
                pv = _dot(p_scr[slot, u], v_ones)
                acc_new = jnp.where(head0, pv[:QBLK, :LANES], pv[QBLK:, :LANES])
                l_new = jnp.where(head0, pv[:QBLK, LANES:], pv[QBLK:, LANES:])
                m_new = m_scr[slot, u]
                pieces = query_pieces(seg, n)
                if d == DILATIONS[0]:
                    results.append((pieces, acc_new, m_new, l_new))
                    continue
                m_old = jnp.concatenate([macc[slab, rows, :] for slab, rows in pieces], axis=0)
                acc_old = jnp.concatenate([oacc[slab, rows, :] for slab, rows in pieces], axis=0)
                l_old = jnp.concatenate([lacc[slab, rows, :] for slab, rows in pieces], axis=0)
                m_tot = jnp.maximum(m_old, m_new)
                w_old = jnp.exp(m_old - m_tot)
                w_new = jnp.exp(m_new - m_tot)
                acc_tot = acc_old * w_old + acc_new * w_new
                l_tot = l_old * w_old + l_new * w_new
                if d == DILATIONS[-1]:
                    results.append((pieces, acc_tot / l_tot, None, None))
                else:
                    results.append((pieces, acc_tot, m_tot, l_tot))
            for (pieces, acc_val, m_val, l_val), u in zip(results, range(ATTN_GROUP)):
                for k, (slab, rows) in enumerate(pieces):
                    part = slice(k * qp, (k + 1) * qp)
                    if d == DILATIONS[-1]:
                        n = (g * ATTN_GROUP + u) % (seq // QBLK)
                        o_ref[0, pl.ds(n * QBLK + k, qp, stride=nslab), :] = acc_val[part]
                    else:
                        oacc[slab, rows, :] = acc_val[part]
                        macc[slab, rows, :] = m_val[part]
                        lacc[slab, rows, :] = l_val[part]

        ngroups = d * nblk // ATTN_GROUP
        stage_scores(0, 0)
        stage_scores(1, 1)
        stage_softmax(0)

        def steady(t, carry):
            g = 2 * t + 2
            stage_scores(g, 0)
            stage_softmax(1)
            stage_output(g - 2, 0)
            stage_scores(g + 1, 1)
            stage_softmax(0)
            stage_output(g - 1, 1)
            return carry

        lax.fori_loop(0, (ngroups - 2) // 2, steady, 0, unroll=ATTN_UNROLL)
        stage_softmax(1)
        stage_output(ngroups - 2, 0)
        stage_output(ngroups - 1, 1)


def _attn_call(q, k, v):
    b, s, d = q.shape
    res, per = RESIDUES, s // RESIDUES
    spec = pl.BlockSpec((1, s, LANES), lambda bi, hp: (bi, 0, hp))
    return pl.pallas_call(
        functools.partial(_attn_kernel, seq=s),
        out_shape=jax.ShapeDtypeStruct(q.shape, F32),
        grid=(b, d // LANES),
        in_specs=[spec, spec, spec],
        out_specs=pl.BlockSpec((1, s, LANES), lambda bi, hp: (bi, 0, hp), pipeline_mode=pl.Buffered(1)),
        scratch_shapes=[
            pltpu.VMEM((res, per, LANES), F32),
            pltpu.VMEM((res, per, LANES), BF16),
            pltpu.VMEM((res, per, LANES), BF16),
            pltpu.VMEM((res, per, LANES), F32),
            pltpu.VMEM((res, per, LANES), F32),
            pltpu.VMEM((res, per, LANES), F32),
            pltpu.VMEM((3 * len(DILATIONS), QBLK, KWIN), F32),
            pltpu.VMEM((2, ATTN_GROUP, 2 * QBLK, KWIN), F32),
            pltpu.VMEM((2, ATTN_GROUP, 2 * QBLK, KWIN), BF16),
            pltpu.VMEM((2, ATTN_GROUP, QBLK, LANES), F32),
            pltpu.VMEM((QBLK * RESIDUES, LANES), F32),
        ],
        compiler_params=_params(("arbitrary", "arbitrary")),
        name="dilated_attn",
    )(q, k, v)


def _rope_tables(seq):
    pos = jnp.arange(seq, dtype=F32)
    inv_freq = ROPE_THETA ** (-jnp.arange(0, ROT_DIM, 2, dtype=F32) / ROT_DIM)
    ang = pos[:, None] * inv_freq[None, :]
    cos, sin = jnp.cos(ang), jnp.sin(ang)
    half = ROT_DIM // 2
    l64 = jnp.arange(LANES) % HEAD_DIM
    f = l64 % half
    cos_t = jnp.where(l64 < ROT_DIM, cos[:, f], 1.0)
    sin_t = jnp.where(l64 < half, -sin[:, f], jnp.where(l64 < ROT_DIM, sin[:, f], 0.0))
    return cos_t, sin_t


def kernel(x, mix_norm_g, mlp_norm_g, mlp_w1, mlp_w2, ab_w_in, a_spatial_w, a_spatial_b, a_vnorm_g, a_vnorm_b, b_conv_w, b_conv_b, b_norm_g, b_norm_b, ab_w_out, c_w_qkv, c_q_norm_g, c_k_norm_g, c_w_out):
    b, s, d = x.shape
    depth = mix_norm_g.shape[0]
    cos_t, sin_t = _rope_tables(s)
    w1s, w2s = mlp_w1.astype(BF16), mlp_w2.astype(BF16)
    wins, wouts = ab_w_in.astype(BF16), ab_w_out.astype(BF16)
    wqkvs, wos = c_w_qkv.astype(BF16), c_w_out.astype(BF16)
    for layer in range(depth):
        i = layer // 2
        g_mix = mix_norm_g[layer][None, :]
        g_mlp = mlp_norm_g[layer][None, :]
        if layer % 2 == 0:
            spw = a_spatial_w[i].astype(BF16).reshape(A_GROUPS // 2, 2 * CHUNK, CHUNK)
            spb = jnp.repeat(a_spatial_b[i].T, A_WIDTH // A_GROUPS, axis=1)
            x = _mixer_mlp_call(
                x.reshape(b * s, d), s, g_mix, wins, i, spw, spb,
                a_vnorm_g[i][None, :], a_vnorm_b[i][None, :], b_conv_w[i], b_conv_b[i][None, :],
                b_norm_g[i][None, :], b_norm_b[i][None, :], wouts, g_mlp, w1s, w2s, layer).reshape(b, s, d)
        else:
            qg = jnp.tile(c_q_norm_g[i], LANES // HEAD_DIM)[None, :]
            kg = jnp.tile(c_k_norm_g[i], LANES // HEAD_DIM)[None, :]
            q, k, v = _qkv_call(x.reshape(b * s, d), g_mix, wqkvs, i, qg, kg, cos_t, sin_t, s)
            a = _attn_call(q.reshape(b, s, d), k.reshape(b, s, d), v.reshape(b, s, d))
            x = _proj_mlp_call(x.reshape(b * s, d), a.reshape(b * s, d), wos, i, g_mlp, w1s, w2s,
                               layer).reshape(b, s, d)
    return x
```

```python
import functools

import jax
import jax.numpy as jnp
from jax import lax
from jax.experimental import pallas as pl
from jax.experimental.pallas import tpu as pltpu

F32 = jnp.float32
BF16 = jnp.bfloat16

D_MODEL = 1024
D_FF = 4 * D_MODEL
A_WIDTH = D_MODEL // 2
B_WIDTH = D_MODEL // 2
A_GROUPS = 8
CHUNK = 128
CONV_WIDTH = 31
HEAD_DIM = 64
ROT_DIM = HEAD_DIM // 4
ROPE_THETA = 500000.0
DILATIONS = (16, 4, 1)
HALF = 64
EPS = 1e-6
NEG = -1e30

LANES = 128
SUBLANES = 8
MXU_COLS = 256
HALO = 16
TM = 512
FF_CHUNK = 512
QBLK = 128
KWIN = 2 * QBLK
ATTN_GROUP = 2
ATTN_UNROLL = 3
RESIDUES = max(DILATIONS)
REGROUP_HOP = 4
VMEM_LIMIT = 56 * 1024 * 1024


def _params(sem, vmem=VMEM_LIMIT):
    return pltpu.CompilerParams(dimension_semantics=sem, vmem_limit_bytes=vmem)


def _rms(x, g):
    ms = jnp.mean(x * x, axis=-1, keepdims=True)
    return x * lax.rsqrt(ms + EPS) * g


def _layernorm(x, g, b):
    mu = jnp.mean(x, axis=-1, keepdims=True)
    xc = x - mu
    var = jnp.mean(xc * xc, axis=-1, keepdims=True)
    return xc * lax.rsqrt(var + EPS) * g + b


def _dot(a, b):
    return jnp.dot(a, b, preferred_element_type=F32)


def _run_interleaved(*parts):
    live = list(parts)
    while live:
        for part in list(live):
            if next(part, StopIteration) is StopIteration:
                live.remove(part)


def _layer_block(stack, index):
    zeros = (0,) * (stack.ndim - 1)
    return pl.BlockSpec((None,) + stack.shape[1:], lambda *_: (index,) + zeros, pipeline_mode=pl.Buffered(1))


def _mlp_steps(x, g, w1_ref, w2_ref):
    h = _rms(x, g).astype(BF16)
    acc = x
    for j in range(D_FF // FF_CHUNK):
        cols = slice(j * FF_CHUNK, (j + 1) * FF_CHUNK)
        a = _dot(h, w1_ref[:, cols])
        a = jnp.square(jnp.maximum(a, 0.0)).astype(BF16)
        acc = acc + _dot(a, w2_ref[cols, :])
        yield acc


def _mlp_body(x, g, w1_ref, w2_ref):
    for acc in _mlp_steps(x, g, w1_ref, w2_ref):
        pass
    return acc


def _proj_mlp_kernel(x_ref, a_ref, wo_ref, g_ref, w1_ref, w2_ref, o_ref):
    x = x_ref[...] + _dot(a_ref[...].astype(BF16), wo_ref[...])
    o_ref[...] = _mlp_body(x, g_ref[...], w1_ref, w2_ref)


def _proj_mlp_call(x2d, a2d, wo_stack, odd_index, g, w1_stack, w2_stack, layer):
    n = x2d.shape[0]
    const = lambda i: (0, 0)
    row = lambda i: (i, 0)
    return pl.pallas_call(
        _proj_mlp_kernel,
        out_shape=jax.ShapeDtypeStruct(x2d.shape, F32),
        grid=(n // TM,),
        in_specs=[
            pl.BlockSpec((TM, D_MODEL), row),
            pl.BlockSpec((TM, D_MODEL), row),
            _layer_block(wo_stack, odd_index),
            pl.BlockSpec((1, D_MODEL), const),
            _layer_block(w1_stack, layer),
            _layer_block(w2_stack, layer),
        ],
        out_specs=pl.BlockSpec((TM, D_MODEL), row),
        compiler_params=_params(("arbitrary",)),
        name="attn_out_proj_mlp",
    )(x2d, a2d, wo_stack, g, w1_stack, w2_stack)


def _mixer_mlp_kernel(xp_ref, x_ref, xn_ref, g_ref, win_ref, spw_ref, spb_ref, vg_ref, vb_ref,
                      cw_ref, cb_ref, cng_ref, cnb_ref, wout_ref, g2_ref, w1_ref, w2_ref,
                      o_ref, gs_ref, sh_ref, xm_ref, *, blocks_per_seq):
    t = pl.program_id(0)
    nsteps = pl.num_programs(0)

    def mlp():
        for acc in _mlp_steps(xm_ref[...], g2_ref[...], w1_ref, w2_ref):
            yield
        o_ref[...] = acc
        yield

    def mixer():
        i = t % blocks_per_seq
        last = blocks_per_seq - 1
        g = g_ref[...]
        x = x_ref[...]
        h = _rms(x, g).astype(BF16)
        h_all = jnp.concatenate(
            [_rms(xp_ref[...], g).astype(BF16), h, _rms(xn_ref[...], g).astype(BF16)], axis=0)

        zb = _dot(h_all, win_ref[:, 2 * A_WIDTH:])
        glu = zb[:, :B_WIDTH] * jax.nn.sigmoid(zb[:, B_WIDTH:])
        row = lax.broadcasted_iota(jnp.int32, glu.shape, 0)
        inside = ((row >= HALO) | (i > 0)) & ((row < TM + HALO) | (i < last))
        gs_ref[...] = jnp.where(inside, glu, 0.0)
        yield
        conv = jnp.broadcast_to(cb_ref[...], (TM, B_WIDTH))
        first_tap = HALO - CONV_WIDTH // 2
        for shift in range(SUBLANES):
            src = gs_ref
            if shift:
                sh_ref[...] = gs_ref[pl.ds(shift, TM + 2 * HALO - SUBLANES), :]
                src = sh_ref
            for base in range(0, 2 * HALO, SUBLANES):
                k = base + shift - first_tap
                if 0 <= k < CONV_WIDTH:
                    conv = conv + cw_ref[k:k + 1, :] * src[pl.ds(base, TM), :]
            if shift % 2 == 1 and shift < SUBLANES - 1:
                yield
        yb = _layernorm(conv, cng_ref[...], cnb_ref[...])
        yb = yb * jax.nn.sigmoid(yb)
        yield

        za = jax.nn.gelu(_dot(h, win_ref[:, :2 * A_WIDTH]))
        u = za[:, :A_WIDTH]
        v = _layernorm(za[:, A_WIDTH:], vg_ref[...], vb_ref[...]).astype(BF16)
        yield
        lane = lax.broadcasted_iota(jnp.int32, (CHUNK, LANES), 1)
        first_group = lane < LANES // 2
        ya_chunks = []
        for c in range(TM // CHUNK):
            vc = v[c * CHUNK:(c + 1) * CHUNK, :]
            parts = []
            for j in range(A_WIDTH // LANES):
                r = _dot(spw_ref[j], vc[:, j * LANES:(j + 1) * LANES])
                parts.append(jnp.where(first_group, r[:CHUNK], r[CHUNK:]))
            sv = jnp.concatenate(parts, axis=1) + spb_ref[...]
            ya_chunks.append(u[c * CHUNK:(c + 1) * CHUNK, :] * sv)
            if c % 2 == 1:
                yield
        ya = jnp.concatenate(ya_chunks, axis=0)

        out = _dot(ya.astype(BF16), wout_ref[:A_WIDTH, :]) + _dot(yb.astype(BF16), wout_ref[A_WIDTH:, :])
        xm_ref[...] = x + out
        yield

    @pl.when(t == 0)
    def _():
        _run_interleaved(mixer())

    @pl.when((t > 0) & (t < nsteps - 1))
    def _():
        _run_interleaved(mlp(), mixer())

    @pl.when(t == nsteps - 1)
    def _():
        _run_interleaved(mlp())


def _mixer_mlp_call(x2d, seq, g, win_stack, even_index, spw, spb, vg, vb, cw, cb, cng, cnb, wout_stack,
                    g2, w1_stack, w2_stack, layer):
    n, d = x2d.shape
    nblk = n // TM
    per = TM // HALO
    const2 = lambda t: (0, 0)
    const3 = lambda t: (0, 0, 0)
    cur = lambda t: jnp.minimum(t, nblk - 1)
    resident = dict(pipeline_mode=pl.Buffered(1))
    return pl.pallas_call(
        functools.partial(_mixer_mlp_kernel, blocks_per_seq=seq // TM),
        out_shape=jax.ShapeDtypeStruct(x2d.shape, F32),
        grid=(nblk + 1,),
        in_specs=[
            pl.BlockSpec((HALO, d), lambda t: (jnp.maximum(cur(t) * per - 1, 0), 0)),
            pl.BlockSpec((TM, d), lambda t: (cur(t), 0)),
            pl.BlockSpec((HALO, d), lambda t: (jnp.minimum((cur(t) + 1) * per, n // HALO - 1), 0)),
            pl.BlockSpec((1, d), const2),
            _layer_block(win_stack, even_index),
            pl.BlockSpec(spw.shape, const3, **resident),
            pl.BlockSpec(spb.shape, const2, **resident),
            pl.BlockSpec((1, A_WIDTH), const2),
            pl.BlockSpec((1, A_WIDTH), const2),
            pl.BlockSpec(cw.shape, const2),
            pl.BlockSpec((1, B_WIDTH), const2),
            pl.BlockSpec((1, B_WIDTH), const2),
            pl.BlockSpec((1, B_WIDTH), const2),
            _layer_block(wout_stack, even_index),
            pl.BlockSpec((1, d), const2),
            _layer_block(w1_stack, layer),
            _layer_block(w2_stack, layer),
        ],
        out_specs=pl.BlockSpec((TM, d), lambda t: (jnp.maximum(t - 1, 0), 0)),
        scratch_shapes=[pltpu.VMEM((TM + 2 * HALO, B_WIDTH), F32),
                        pltpu.VMEM((TM + 2 * HALO - SUBLANES, B_WIDTH), F32),
                        pltpu.VMEM((TM, D_MODEL), F32)],
        compiler_params=_params(("arbitrary",)),
        name="mixer_ab_mlp",
    )(x2d, x2d, x2d, g, win_stack, spw, spb, vg, vb, cw, cb, cng, cnb, wout_stack, g2, w1_stack, w2_stack)


def _qkv_kernel(x_ref, g_ref, w_ref, qg_ref, kg_ref, cos_ref, sin_ref, q_ref, k_ref, v_ref, raw_ref):
    i = pl.program_id(0)
    nsteps = pl.num_programs(0)

    def project():
        h = _rms(x_ref[...], g_ref[...]).astype(BF16)
        for part in reversed(range(3)):
            cols = slice(part * D_MODEL, (part + 1) * D_MODEL)
            raw_ref[:, cols] = _dot(h, w_ref[:, cols])
            yield

    def finish():
        lane = lax.broadcasted_iota(jnp.int32, (TM, LANES), 1)
        low = (lane % HEAD_DIM) < ROT_DIM // 2
        cos = cos_ref[...]
        sin = sin_ref[...]
        hr = lax.broadcasted_iota(jnp.int32, (MXU_COLS, MXU_COLS), 0) // HEAD_DIM
        hc = lax.broadcasted_iota(jnp.int32, (MXU_COLS, MXU_COLS), 1) // HEAD_DIM
        same_head = jnp.where(hr == hc, 1.0, 0.0).astype(BF16)

        def head_norm_rope(t4, gain):
            ms = _dot((t4 * t4).astype(BF16), same_head) * (1.0 / HEAD_DIM)
            outs = []
            for half in range(MXU_COLS // LANES):
                sl = slice(half * LANES, (half + 1) * LANES)
                t = t4[:, sl] * lax.rsqrt(ms[:, sl] + EPS) * gain
                partner = jnp.where(low, pltpu.roll(t, LANES - ROT_DIM // 2, 1), pltpu.roll(t, ROT_DIM // 2, 1))
                outs.append(t * cos + partner * sin)
            return jnp.concatenate(outs, axis=1)

        v_ref[...] = raw_ref[:, 2 * D_MODEL:]
        yield
        for j in range(D_MODEL // MXU_COLS):
            cols = slice(j * MXU_COLS, (j + 1) * MXU_COLS)
            kcols = slice(D_MODEL + j * MXU_COLS, D_MODEL + (j + 1) * MXU_COLS)
            k_ref[:, cols] = head_norm_rope(raw_ref[:, kcols], kg_ref[...])
        yield
        for j in range(D_MODEL // MXU_COLS):
            cols = slice(j * MXU_COLS, (j + 1) * MXU_COLS)
            q_ref[:, cols] = head_norm_rope(raw_ref[:, cols], qg_ref[...]) * (HEAD_DIM ** -0.5)
        yield

    @pl.when(i == 0)
    def _():
        _run_interleaved(project())

    @pl.when((i > 0) & (i < nsteps - 1))
    def _():
        _run_interleaved(finish(), project())

    @pl.when(i == nsteps - 1)
    def _():
        _run_interleaved(finish())


def _qkv_call(x2d, g, w_stack, odd_index, qg, kg, cos_t, sin_t, seq):
    n = x2d.shape[0]
    nblk = n // TM
    pos_blocks = seq // TM
    const = lambda i: (0, 0)
    lagged = lambda i: (jnp.maximum(i - 1, 0), 0)
    lagged_pos = lambda i: (jnp.maximum(i - 1, 0) % pos_blocks, 0)
    out = jax.ShapeDtypeStruct(x2d.shape, F32)
    return pl.pallas_call(
        _qkv_kernel,
        out_shape=(out, out, out),
        grid=(nblk + 1,),
        in_specs=[
            pl.BlockSpec((TM, D_MODEL), lambda i: (jnp.minimum(i, nblk - 1), 0)),
            pl.BlockSpec((1, D_MODEL), const),
            _layer_block(w_stack, odd_index),
            pl.BlockSpec((1, LANES), const),
            pl.BlockSpec((1, LANES), const),
            pl.BlockSpec((TM, LANES), lagged_pos),
            pl.BlockSpec((TM, LANES), lagged_pos),
        ],
        out_specs=(pl.BlockSpec((TM, D_MODEL), lagged),) * 3,
        scratch_shapes=[pltpu.VMEM((TM, 3 * D_MODEL), F32)],
        compiler_params=_params(("arbitrary",)),
        name="qkv",
    )(x2d, g, w_stack, qg, kg, cos_t, sin_t)


def _attn_kernel(qn_ref, kn_ref, vn_ref, o_ref, q_ref, k_ref, v_ref, oacc, macc, lacc, bias_ref,
                 s_scr, p_scr, m_scr, stage, *, seq):
    per = seq // RESIDUES

    def regroup(c, carry):
        hop = REGROUP_HOP
        span = QBLK * RESIDUES
        seg = span // hop
        for src_ref, dst_ref in ((qn_ref, q_ref), (kn_ref, k_ref), (vn_ref, v_ref)):
            for r1 in range(hop):
                for part in range(seg // QBLK):
                    rows = pl.ds(c * span + r1 + part * QBLK * hop, QBLK, stride=hop)
                    stage[pl.ds(r1 * seg + part * QBLK, QBLK), :] = src_ref[0, rows, :]
            for r in range(RESIDUES):
                r1, r2 = r % hop, r // hop
                val = stage[pl.ds(r1 * seg + r2, QBLK, stride=RESIDUES // hop), :]
                dst_ref[r, pl.ds(pl.multiple_of(c * QBLK, QBLK), QBLK), :] = val.astype(dst_ref.dtype)
        return carry

    lax.fori_loop(0, per // QBLK, regroup, 0)
    lane = lax.broadcasted_iota(jnp.int32, (QBLK, LANES), 1)
    head0 = lane < HEAD_DIM
    ones_keys = jnp.ones((KWIN, LANES), BF16)
    row = lax.broadcasted_iota(jnp.int32, (QBLK, KWIN), 0)
    col = lax.broadcasted_iota(jnp.int32, (QBLK, KWIN), 1)
    for pat, d in enumerate(DILATIONS):
        nslab = RESIDUES // d
        qp, kp = QBLK // nslab, KWIN // nslab
        qpos = nslab * (row % qp) + row // qp
        kpos = col if d == 1 else nslab * (col % kp) + col // kp
        for which, shift in enumerate((HALF, 0, 2 * HALF)):
            diff = kpos - qpos - shift
            bias_ref[3 * pat + which] = jnp.where((diff >= -HALF) & (diff <= HALF), 0.0, NEG)

    for pat, d in enumerate(DILATIONS):
        nslab = RESIDUES // d
        qp, kp = QBLK // nslab, KWIN // nslab
        nblk = seq // d // QBLK

        def locate(idx, nblk=nblk):
            return idx // nblk, idx % nblk

        def slab_rows(seg, start, size, d=d, nslab=nslab):
            return [(seg + d * j, pl.ds(start, size)) for j in range(nslab)]

        def gather(ref, pieces):
            parts = [ref[slab, rows, :] for slab, rows in pieces]
            return parts[0] if len(parts) == 1 else jnp.concatenate(parts, axis=0)

        def query_pieces(seg, n, qp=qp):
            return slab_rows(seg, pl.multiple_of(n * qp, qp), qp)

        def key_block(ref, ref_n, seg, n, d=d, qp=qp, kp=kp):
            if d == 1:
                start = jnp.clip(n * QBLK - HALF, 0, seq - KWIN)
                return ref_n[0, pl.ds(pl.multiple_of(start, HALF), KWIN), :].astype(BF16)
            start = jnp.clip(n * qp - kp // 4, 0, per - kp)
            return gather(ref, slab_rows(seg, pl.multiple_of(start, kp // 4), kp))

        def stage_scores(g, slot, pat=pat, nblk=nblk):
            for u in range(ATTN_GROUP):
                seg, n = locate(g * ATTN_GROUP + u)
                qb = gather(q_ref, query_pieces(seg, n)).astype(BF16)
                kb = key_block(k_ref, kn_ref, seg, n)
                bias = bias_ref[3 * pat + jnp.where(n == 0, 1, jnp.where(n == nblk - 1, 2, 0))]
                zero = jnp.zeros_like(qb)
                qq = jnp.concatenate([jnp.where(head0, qb, zero), jnp.where(head0, zero, qb)], axis=0)
                s = lax.dot_general(qq, kb, (((1,), (1,)), ((), ())), preferred_element_type=F32)
                s_scr[slot, u, :QBLK, :] = s[:QBLK] + bias
                s_scr[slot, u, QBLK:, :] = s[QBLK:] + bias

        def stage_softmax(slot):
            for u in range(ATTN_GROUP):
                s = s_scr[slot, u]
                m = jnp.max(s, axis=-1, keepdims=True)
                p_scr[slot, u] = jnp.exp(s - m).astype(BF16)
                m_scr[slot, u] = jnp.where(head0, m[:QBLK], m[QBLK:])

        def stage_output(g, slot, d=d, qp=qp, nslab=nslab):
            results = []
            for u in range(ATTN_GROUP):
                seg, n = locate(g * ATTN_GROUP + u)
                v_ones = jnp.concatenate([key_block(v_ref, vn_ref, seg, n), ones_keys], axis=1)
                pv = _dot(p_scr[slot, u], v_ones)
                acc_new = jnp.where(head0, pv[:QBLK, :LANES], pv[QBLK:, :LANES])
                l_new = jnp.where(head0, pv[:QBLK, LANES:], pv[QBLK:, LANES:])
                m_new = m_scr[slot, u]
                pieces = query_pieces(seg, n)
                if d == DILATIONS[0]:
                    results.append((pieces, acc_new, m_new, l_new))
                    continue
                m_old = jnp.concatenate([macc[slab, rows, :] for slab, rows in pieces], axis=0)
                acc_old = jnp.concatenate([oacc[slab, rows, :] for slab, rows in pieces], axis=0)
                l_old = jnp.concatenate([lacc[slab, rows, :] for slab, rows in pieces], axis=0)
                m_tot = jnp.maximum(m_old, m_new)
                w_old = jnp.exp(m_old - m_tot)
                w_new = jnp.exp(m_new - m_tot)
                acc_tot = acc_old * w_old + acc_new * w_new
                l_tot = l_old * w_old + l_new * w_new
                if d == DILATIONS[-1]:
                    results.append((pieces, acc_tot / l_tot, None, None))
                else:
                    results.append((pieces, acc_tot, m_tot, l_tot))
            for (pieces, acc_val, m_val, l_val), u in zip(results, range(ATTN_GROUP)):
                for k, (slab, rows) in enumerate(pieces):
                    part = slice(k * qp, (k + 1) * qp)
                    if d == DILATIONS[-1]:
                        n = (g * ATTN_GROUP + u) % (seq // QBLK)
                        o_ref[0, pl.ds(n * QBLK + k, qp, stride=nslab), :] = acc_val[part]
                    else:
                        oacc[slab, rows, :] = acc_val[part]
                        macc[slab, rows, :] = m_val[part]
                        lacc[slab, rows, :] = l_val[part]

        ngroups = d * nblk // ATTN_GROUP
        stage_scores(0, 0)
        stage_scores(1, 1)
        stage_softmax(0)

        def steady(t, carry):
            g = 2 * t + 2
            stage_scores(g, 0)
            stage_softmax(1)
            stage_output(g - 2, 0)
            stage_scores(g + 1, 1)
            stage_softmax(0)
            stage_output(g - 1, 1)
            return carry

        lax.fori_loop(0, (ngroups - 2) // 2, steady, 0, unroll=ATTN_UNROLL)
        stage_softmax(1)
        stage_output(ngroups - 2, 0)
        stage_output(ngroups - 1, 1)


def _attn_call(q, k, v):
    b, s, d = q.shape
    res, per = RESIDUES, s // RESIDUES
    spec = pl.BlockSpec((1, s, LANES), lambda bi, hp: (bi, 0, hp))
    return pl.pallas_call(
        functools.partial(_attn_kernel, seq=s),
        out_shape=jax.ShapeDtypeStruct(q.shape, F32),
        grid=(b, d // LANES),
        in_specs=[spec, spec, spec],
        out_specs=pl.BlockSpec((1, s, LANES), lambda bi, hp: (bi, 0, hp), pipeline_mode=pl.Buffered(1)),
        scratch_shapes=[
            pltpu.VMEM((res, per, LANES), F32),
            pltpu.VMEM((res, per, LANES), BF16),
            pltpu.VMEM((res, per, LANES), BF16),
            pltpu.VMEM((res, per, LANES), F32),
            pltpu.VMEM((res, per, LANES), F32),
            pltpu.VMEM((res, per, LANES), F32),
            pltpu.VMEM((3 * len(DILATIONS), QBLK, KWIN), F32),
            pltpu.VMEM((2, ATTN_GROUP, 2 * QBLK, KWIN), F32),
            pltpu.VMEM((2, ATTN_GROUP, 2 * QBLK, KWIN), BF16),
            pltpu.VMEM((2, ATTN_GROUP, QBLK, LANES), F32),
            pltpu.VMEM((QBLK * RESIDUES, LANES), F32),
        ],
        compiler_params=_params(("arbitrary", "arbitrary")),
        name="dilated_attn",
    )(q, k, v)


def _rope_tables(seq):
    pos = jnp.arange(seq, dtype=F32)
    inv_freq = ROPE_THETA ** (-jnp.arange(0, ROT_DIM, 2, dtype=F32) / ROT_DIM)
    ang = pos[:, None] * inv_freq[None, :]
    cos, sin = jnp.cos(ang), jnp.sin(ang)
    half = ROT_DIM // 2
    l64 = jnp.arange(LANES) % HEAD_DIM
    f = l64 % half
    cos_t = jnp.where(l64 < ROT_DIM, cos[:, f], 1.0)
    sin_t = jnp.where(l64 < half, -sin[:, f], jnp.where(l64 < ROT_DIM, sin[:, f], 0.0))
    return cos_t, sin_t


def kernel(x, mix_norm_g, mlp_norm_g, mlp_w1, mlp_w2, ab_w_in, a_spatial_w, a_spatial_b, a_vnorm_g, a_vnorm_b, b_conv_w, b_conv_b, b_norm_g, b_norm_b, ab_w_out, c_w_qkv, c_q_norm_g, c_k_norm_g, c_w_out):
    b, s, d = x.shape
    depth = mix_norm_g.shape[0]
    cos_t, sin_t = _rope_tables(s)
    w1s, w2s = mlp_w1.astype(BF16), mlp_w2.astype(BF16)
    wins, wouts = ab_w_in.astype(BF16), ab_w_out.astype(BF16)
    wqkvs, wos = c_w_qkv.astype(BF16), c_w_out.astype(BF16)
    for layer in range(depth):
        i = layer // 2
        g_mix = mix_norm_g[layer][None, :]
        g_mlp = mlp_norm_g[layer][None, :]
        if layer % 2 == 0:
            spw = a_spatial_w[i].astype(BF16).reshape(A_GROUPS // 2, 2 * CHUNK, CHUNK)
            spb = jnp.repeat(a_spatial_b[i].T, A_WIDTH // A_GROUPS, axis=1)
            x = _mixer_mlp_call(
                x.reshape(b * s, d), s, g_mix, wins, i, spw, spb,
                a_vnorm_g[i][None, :], a_vnorm_b[i][None, :], b_conv_w[i], b_conv_b[i][None, :],
                b_norm_g[i][None, :], b_norm_b[i][None, :], wouts, g_mlp, w1s, w2s, layer).reshape(b, s, d)
        else:
            qg = jnp.tile(c_q_norm_g[i], LANES // HEAD_DIM)[None, :]
            kg = jnp.tile(c_k_norm_g[i], LANES // HEAD_DIM)[None, :]
            q, k, v = _qkv_call(x.reshape(b * s, d), g_mix, wqkvs, i, qg, kg, cos_t, sin_t, s)
            a = _attn_call(q.reshape(b, s, d), k.reshape(b, s, d), v.reshape(b, s, d))
            x = _proj_mlp_call(x.reshape(b * s, d), a.reshape(b * s, d), wos, i, g_mlp, w1s, w2s,
                               layer).reshape(b, s, d)
    return x
```

```python
import functools

import jax
import jax.numpy as jnp
from jax import lax
from jax.experimental import pallas as pl
from jax.experimental.pallas import tpu as pltpu

F32 = jnp.float32
BF16 = jnp.bfloat16

D_MODEL = 1024
D_FF = 4 * D_MODEL
A_WIDTH = D_MODEL // 2
B_WIDTH = D_MODEL // 2
A_GROUPS = 8
CHUNK = 128
CONV_WIDTH = 31
HEAD_DIM = 64
ROT_DIM = HEAD_DIM // 4
ROPE_THETA = 500000.0
DILATIONS = (16, 4, 1)
HALF = 64
EPS = 1e-6
NEG = -1e30

LANES = 128
SUBLANES = 8
MXU_COLS = 256
HALO = 16
TM = 512
FF_CHUNK = 512
QBLK = 128
KWIN = 2 * QBLK
ATTN_GROUP = 1
ATTN_UNROLL = 15
RESIDUES = max(DILATIONS)
REGROUP_HOP = 4
VMEM_LIMIT = 56 * 1024 * 1024


def _params(sem, vmem=VMEM_LIMIT):
    return pltpu.CompilerParams(dimension_semantics=sem, vmem_limit_bytes=vmem)


def _rms(x, g):
    ms = jnp.mean(x * x, axis=-1, keepdims=True)
    return x * lax.rsqrt(ms + EPS) * g


def _layernorm(x, g, b):
    mu = jnp.mean(x, axis=-1, keepdims=True)
    xc = x - mu
    var = jnp.mean(xc * xc, axis=-1, keepdims=True)
    return xc * lax.rsqrt(var + EPS) * g + b


def _dot(a, b):
    return jnp.dot(a, b, preferred_element_type=F32)


def _run_interleaved(*parts):
    live = list(parts)
    while live:
        for part in list(live):
            if next(part, StopIteration) is StopIteration:
                live.remove(part)


def _layer_block(stack, index):
    zeros = (0,) * (stack.ndim - 1)
    return pl.BlockSpec((None,) + stack.shape[1:], lambda *_: (index,) + zeros, pipeline_mode=pl.Buffered(1))


def _mlp_steps(x, g, w1_ref, w2_ref):
    h = _rms(x, g).astype(BF16)
    acc = x
    for j in range(D_FF // FF_CHUNK):
        cols = slice(j * FF_CHUNK, (j + 1) * FF_CHUNK)
        a = _dot(h, w1_ref[:, cols])
        a = jnp.square(jnp.maximum(a, 0.0)).astype(BF16)
        acc = acc + _dot(a, w2_ref[cols, :])
        yield acc


def _mlp_body(x, g, w1_ref, w2_ref):
    for acc in _mlp_steps(x, g, w1_ref, w2_ref):
        pass
    return acc


def _proj_mlp_kernel(x_ref, a_ref, wo_ref, g_ref, w1_ref, w2_ref, o_ref):
    x = x_ref[...] + _dot(a_ref[...].astype(BF16), wo_ref[...])
    o_ref[...] = _mlp_body(x, g_ref[...], w1_ref, w2_ref)


def _proj_mlp_call(x2d, a2d, wo_stack, odd_index, g, w1_stack, w2_stack, layer):
    n = x2d.shape[0]
    const = lambda i: (0, 0)
    row = lambda i: (i, 0)
    return pl.pallas_call(
        _proj_mlp_kernel,
        out_shape=jax.ShapeDtypeStruct(x2d.shape, F32),
        grid=(n // TM,),
        in_specs=[
            pl.BlockSpec((TM, D_MODEL), row),
            pl.BlockSpec((TM, D_MODEL), row),
            _layer_block(wo_stack, odd_index),
            pl.BlockSpec((1, D_MODEL), const),
            _layer_block(w1_stack, layer),
            _layer_block(w2_stack, layer),
        ],
        out_specs=pl.BlockSpec((TM, D_MODEL), row),
        compiler_params=_params(("arbitrary",)),
        name="attn_out_proj_mlp",
    )(x2d, a2d, wo_stack, g, w1_stack, w2_stack)


def _mixer_mlp_kernel(xp_ref, x_ref, xn_ref, g_ref, win_ref, spw_ref, spb_ref, vg_ref, vb_ref,
                      cw_ref, cb_ref, cng_ref, cnb_ref, wout_ref, g2_ref, w1_ref, w2_ref,
                      o_ref, gs_ref, sh_ref, xm_ref, *, blocks_per_seq):
    t = pl.program_id(0)
    nsteps = pl.num_programs(0)

    def mlp():
        for acc in _mlp_steps(xm_ref[...], g2_ref[...], w1_ref, w2_ref):
            yield
        o_ref[...] = acc
        yield

    def mixer():
        i = t % blocks_per_seq
        last = blocks_per_seq - 1
        g = g_ref[...]
        x = x_ref[...]
        h = _rms(x, g).astype(BF16)
        h_all = jnp.concatenate(
            [_rms(xp_ref[...], g).astype(BF16), h, _rms(xn_ref[...], g).astype(BF16)], axis=0)

        zb = _dot(h_all, win_ref[:, 2 * A_WIDTH:])
        glu = zb[:, :B_WIDTH] * jax.nn.sigmoid(zb[:, B_WIDTH:])
        row = lax.broadcasted_iota(jnp.int32, glu.shape, 0)
        inside = ((row >= HALO) | (i > 0)) & ((row < TM + HALO) | (i < last))
        gs_ref[...] = jnp.where(inside, glu, 0.0)
        yield
        conv = jnp.broadcast_to(cb_ref[...], (TM, B_WIDTH))
        first_tap = HALO - CONV_WIDTH // 2
        for shift in range(SUBLANES):
            src = gs_ref
            if shift:
                sh_ref[...] = gs_ref[pl.ds(shift, TM + 2 * HALO - SUBLANES), :]
                src = sh_ref
            for base in range(0, 2 * HALO, SUBLANES):
                k = base + shift - first_tap
                if 0 <= k < CONV_WIDTH:
                    conv = conv + cw_ref[k:k + 1, :] * src[pl.ds(base, TM), :]
            if shift % 2 == 1 and shift < SUBLANES - 1:
                yield
        yb = _layernorm(conv, cng_ref[...], cnb_ref[...])
        yb = yb * jax.nn.sigmoid(yb)
        yield

        za = jax.nn.gelu(_dot(h, win_ref[:, :2 * A_WIDTH]))
        u = za[:, :A_WIDTH]
        v = _layernorm(za[:, A_WIDTH:], vg_ref[...], vb_ref[...]).astype(BF16)
        yield
        lane = lax.broadcasted_iota(jnp.int32, (CHUNK, LANES), 1)
        first_group = lane < LANES // 2
        ya_chunks = []
        for c in range(TM // CHUNK):
            vc = v[c * CHUNK:(c + 1) * CHUNK, :]
            parts = []
            for j in range(A_WIDTH // LANES):
                r = _dot(spw_ref[j], vc[:, j * LANES:(j + 1) * LANES])
                parts.append(jnp.where(first_group, r[:CHUNK], r[CHUNK:]))
            sv = jnp.concatenate(parts, axis=1) + spb_ref[...]
            ya_chunks.append(u[c * CHUNK:(c + 1) * CHUNK, :] * sv)
            if c % 2 == 1:
                yield
        ya = jnp.concatenate(ya_chunks, axis=0)

        out = _dot(ya.astype(BF16), wout_ref[:A_WIDTH, :]) + _dot(yb.astype(BF16), wout_ref[A_WIDTH:, :])
        xm_ref[...] = x + out
        yield

    @pl.when(t == 0)
    def _():
        _run_interleaved(mixer())

    @pl.when((t > 0) & (t < nsteps - 1))
    def _():
        _run_interleaved(mlp(), mixer())

    @pl.when(t == nsteps - 1)
    def _():
        _run_interleaved(mlp())


def _mixer_mlp_call(x2d, seq, g, win_stack, even_index, spw, spb, vg, vb, cw, cb, cng, cnb, wout_stack,
                    g2, w1_stack, w2_stack, layer):
    n, d = x2d.shape
    nblk = n // TM
    per = TM // HALO
    const2 = lambda t: (0, 0)
    const3 = lambda t: (0, 0, 0)
    cur = lambda t: jnp.minimum(t, nblk - 1)
    resident = dict(pipeline_mode=pl.Buffered(1))
    return pl.pallas_call(
        functools.partial(_mixer_mlp_kernel, blocks_per_seq=seq // TM),
        out_shape=jax.ShapeDtypeStruct(x2d.shape, F32),
        grid=(nblk + 1,),
        in_specs=[
            pl.BlockSpec((HALO, d), lambda t: (jnp.maximum(cur(t) * per - 1, 0), 0)),
            pl.BlockSpec((TM, d), lambda t: (cur(t), 0)),
            pl.BlockSpec((HALO, d), lambda t: (jnp.minimum((cur(t) + 1) * per, n // HALO - 1), 0)),
            pl.BlockSpec((1, d), const2),
            _layer_block(win_stack, even_index),
            pl.BlockSpec(spw.shape, const3, **resident),
            pl.BlockSpec(spb.shape, const2, **resident),
            pl.BlockSpec((1, A_WIDTH), const2),
            pl.BlockSpec((1, A_WIDTH), const2),
            pl.BlockSpec(cw.shape, const2),
            pl.BlockSpec((1, B_WIDTH), const2),
            pl.BlockSpec((1, B_WIDTH), const2),
            pl.BlockSpec((1, B_WIDTH), const2),
            _layer_block(wout_stack, even_index),
            pl.BlockSpec((1, d), const2),
            _layer_block(w1_stack, layer),
            _layer_block(w2_stack, layer),
        ],
        out_specs=pl.BlockSpec((TM, d), lambda t: (jnp.maximum(t - 1, 0), 0)),
        scratch_shapes=[pltpu.VMEM((TM + 2 * HALO, B_WIDTH), F32),
                        pltpu.VMEM((TM + 2 * HALO - SUBLANES, B_WIDTH), F32),
                        pltpu.VMEM((TM, D_MODEL), F32)],
        compiler_params=_params(("arbitrary",)),
        name="mixer_ab_mlp",
    )(x2d, x2d, x2d, g, win_stack, spw, spb, vg, vb, cw, cb, cng, cnb, wout_stack, g2, w1_stack, w2_stack)


def _qkv_kernel(x_ref, g_ref, w_ref, qg_ref, kg_ref, cos_ref, sin_ref, q_ref, k_ref, v_ref, raw_ref):
    i = pl.program_id(0)
    nsteps = pl.num_programs(0)

    def project():
        h = _rms(x_ref[...], g_ref[...]).astype(BF16)
        for part in reversed(range(3)):
            cols = slice(part * D_MODEL, (part + 1) * D_MODEL)
            raw_ref[:, cols] = _dot(h, w_ref[:, cols])
            yield

    def finish():
        lane = lax.broadcasted_iota(jnp.int32, (TM, LANES), 1)
        low = (lane % HEAD_DIM) < ROT_DIM // 2
        cos = cos_ref[...]
        sin = sin_ref[...]
        hr = lax.broadcasted_iota(jnp.int32, (MXU_COLS, MXU_COLS), 0) // HEAD_DIM
        hc = lax.broadcasted_iota(jnp.int32, (MXU_COLS, MXU_COLS), 1) // HEAD_DIM
        same_head = jnp.where(hr == hc, 1.0, 0.0).astype(BF16)

        def head_norm_rope(t4, gain):
            ms = _dot((t4 * t4).astype(BF16), same_head) * (1.0 / HEAD_DIM)
            outs = []
            for half in range(MXU_COLS // LANES):
                sl = slice(half * LANES, (half + 1) * LANES)
                t = t4[:, sl] * lax.rsqrt(ms[:, sl] + EPS) * gain
                partner = jnp.where(low, pltpu.roll(t, LANES - ROT_DIM // 2, 1), pltpu.roll(t, ROT_DIM // 2, 1))
                outs.append(t * cos + partner * sin)
            return jnp.concatenate(outs, axis=1)

        v_ref[...] = raw_ref[:, 2 * D_MODEL:]
        yield
        for j in range(D_MODEL // MXU_COLS):
            cols = slice(j * MXU_COLS, (j + 1) * MXU_COLS)
            kcols = slice(D_MODEL + j * MXU_COLS, D_MODEL + (j + 1) * MXU_COLS)
            k_ref[:, cols] = head_norm_rope(raw_ref[:, kcols], kg_ref[...])
        yield
        for j in range(D_MODEL // MXU_COLS):
            cols = slice(j * MXU_COLS, (j + 1) * MXU_COLS)
            q_ref[:, cols] = head_norm_rope(raw_ref[:, cols], qg_ref[...]) * (HEAD_DIM ** -0.5)
        yield

    @pl.when(i == 0)
    def _():
        _run_interleaved(project())

    @pl.when((i > 0) & (i < nsteps - 1))
    def _():
        _run_interleaved(finish(), project())

    @pl.when(i == nsteps - 1)
    def _():
        _run_interleaved(finish())


def _qkv_call(x2d, g, w_stack, odd_index, qg, kg, cos_t, sin_t, seq):
    n = x2d.shape[0]
    nblk = n // TM
    pos_blocks = seq // TM
    const = lambda i: (0, 0)
    lagged = lambda i: (jnp.maximum(i - 1, 0), 0)
    lagged_pos = lambda i: (jnp.maximum(i - 1, 0) % pos_blocks, 0)
    out = jax.ShapeDtypeStruct(x2d.shape, F32)
    return pl.pallas_call(
        _qkv_kernel,
        out_shape=(out, out, out),
        grid=(nblk + 1,),
        in_specs=[
            pl.BlockSpec((TM, D_MODEL), lambda i: (jnp.minimum(i, nblk - 1), 0)),
            pl.BlockSpec((1, D_MODEL), const),
            _layer_block(w_stack, odd_index),
            pl.BlockSpec((1, LANES), const),
            pl.BlockSpec((1, LANES), const),
            pl.BlockSpec((TM, LANES), lagged_pos),
            pl.BlockSpec((TM, LANES), lagged_pos),
        ],
        out_specs=(pl.BlockSpec((TM, D_MODEL), lagged),) * 3,
        scratch_shapes=[pltpu.VMEM((TM, 3 * D_MODEL), F32)],
        compiler_params=_params(("arbitrary",)),
        name="qkv",
    )(x2d, g, w_stack, qg, kg, cos_t, sin_t)


def _attn_kernel(qn_ref, kn_ref, vn_ref, o_ref, q_ref, k_ref, v_ref, oacc, macc, lacc, bias_ref,
                 s_scr, p_scr, m_scr, stage, *, seq):
    per = seq // RESIDUES

    def regroup(c, carry):
        hop = REGROUP_HOP
        span = QBLK * RESIDUES
        seg = span // hop
        for src_ref, dst_ref in ((qn_ref, q_ref), (kn_ref, k_ref), (vn_ref, v_ref)):
            for r1 in range(hop):
                for part in range(seg // QBLK):
                    rows = pl.ds(c * span + r1 + part * QBLK * hop, QBLK, stride=hop)
                    stage[pl.ds(r1 * seg + part * QBLK, QBLK), :] = src_ref[0, rows, :]
            for r in range(RESIDUES):
                r1, r2 = r % hop, r // hop
                val = stage[pl.ds(r1 * seg + r2, QBLK, stride=RESIDUES // hop), :]
                dst_ref[r, pl.ds(pl.multiple_of(c * QBLK, QBLK), QBLK), :] = val.astype(dst_ref.dtype)
        return carry

    lax.fori_loop(0, per // QBLK, regroup, 0)
    lane = lax.broadcasted_iota(jnp.int32, (QBLK, LANES), 1)
    head0 = lane < HEAD_DIM
    ones_keys = jnp.ones((KWIN, LANES), BF16)
    row = lax.broadcasted_iota(jnp.int32, (QBLK, KWIN), 0)
    col = lax.broadcasted_iota(jnp.int32, (QBLK, KWIN), 1)
    for pat, d in enumerate(DILATIONS):
        nslab = RESIDUES // d
        qp, kp = QBLK // nslab, KWIN // nslab
        qpos = nslab * (row % qp) + row // qp
        kpos = col if d == 1 else nslab * (col % kp) + col // kp
        for which, shift in enumerate((HALF, 0, 2 * HALF)):
            diff = kpos - qpos - shift
            bias_ref[3 * pat + which] = jnp.where((diff >= -HALF) & (diff <= HALF), 0.0, NEG)

    for pat, d in enumerate(DILATIONS):
        nslab = RESIDUES // d
        qp, kp = QBLK // nslab, KWIN // nslab
        nblk = seq // d // QBLK

        def locate(idx, nblk=nblk):
            return idx // nblk, idx % nblk

        def slab_rows(seg, start, size, d=d, nslab=nslab):
            return [(seg + d * j, pl.ds(start, size)) for j in range(nslab)]

        def gather(ref, pieces):
            parts = [ref[slab, rows, :] for slab, rows in pieces]
            return parts[0] if len(parts) == 1 else jnp.concatenate(parts, axis=0)

        def query_pieces(seg, n, qp=qp):
            return slab_rows(seg, pl.multiple_of(n * qp, qp), qp)

        def key_block(ref, ref_n, seg, n, d=d, qp=qp, kp=kp):
            if d == 1:
                start = jnp.clip(n * QBLK - HALF, 0, seq - KWIN)
                return ref_n[0, pl.ds(pl.multiple_of(start, HALF), KWIN), :].astype(BF16)
            start = jnp.clip(n * qp - kp // 4, 0, per - kp)
            return gather(ref, slab_rows(seg, pl.multiple_of(start, kp // 4), kp))

        def stage_scores(g, slot, pat=pat, nblk=nblk):
            for u in range(ATTN_GROUP):
                seg, n = locate(g * ATTN_GROUP + u)
                qb = gather(q_ref, query_pieces(seg, n)).astype(BF16)
                kb = key_block(k_ref, kn_ref, seg, n)
                bias = bias_ref[3 * pat + jnp.where(n == 0, 1, jnp.where(n == nblk - 1, 2, 0))]
                zero = jnp.zeros_like(qb)
                qq = jnp.concatenate([jnp.where(head0, qb, zero), jnp.where(head0, zero, qb)], axis=0)
                s = lax.dot_general(qq, kb, (((1,), (1,)), ((), ())), preferred_element_type=F32)
                s_scr[slot, u, :QBLK, :] = s[:QBLK] + bias
                s_scr[slot, u, QBLK:, :] = s[QBLK:] + bias

        def stage_softmax(slot):
            for u in range(ATTN_GROUP):
                s = s_scr[slot, u]
                m = jnp.max(s, axis=-1, keepdims=True)
                p_scr[slot, u] = jnp.exp(s - m).astype(BF16)
                m_scr[slot, u] = jnp.where(head0, m[:QBLK], m[QBLK:])

        def stage_output(g, slot, d=d, qp=qp, nslab=nslab):
            results = []
            for u in range(ATTN_GROUP):
                seg, n = locate(g * ATTN_GROUP + u)
                v_ones = jnp.concatenate([key_block(v_ref, vn_ref, seg, n), ones_keys], axis=1)
                pv = _dot(p_scr[slot, u], v_ones)
                acc_new = jnp.where(head0, pv[:QBLK, :LANES], pv[QBLK:, :LANES])
                l_new = jnp.where(head0, pv[:QBLK, LANES:], pv[QBLK:, LANES:])
                m_new = m_scr[slot, u]
                pieces = query_pieces(seg, n)
                if d == DILATIONS[0]:
                    results.append((pieces, acc_new, m_new, l_new))
                    continue
                m_old = jnp.concatenate([macc[slab, rows, :] for slab, rows in pieces], axis=0)
                acc_old = jnp.concatenate([oacc[slab, rows, :] for slab, rows in pieces], axis=0)
                l_old = jnp.concatenate([lacc[slab, rows, :] for slab, rows in pieces], axis=0)
                m_tot = jnp.maximum(m_old, m_new)
                w_old = jnp.exp(m_old - m_tot)
                w_new = jnp.exp(m_new - m_tot)
                acc_tot = acc_old * w_old + acc_new * w_new
                l_tot = l_old * w_old + l_new * w_new
                if d == DILATIONS[-1]:
                    results.append((pieces, acc_tot / l_tot, None, None))
                else:
                    results.append((pieces, acc_tot, m_tot, l_tot))
            for (pieces, acc_val, m_val, l_val), u in zip(results, range(ATTN_GROUP)):
                for k, (slab, rows) in enumerate(pieces):
                    part = slice(k * qp, (k + 1) * qp)
                    if d == DILATIONS[-1]:
                        n = (g * ATTN_GROUP + u) % (seq // QBLK)
                        o_ref[0, pl.ds(n * QBLK + k, qp, stride=nslab), :] = acc_val[part]
                    else:
                        oacc[slab, rows, :] = acc_val[part]
                        macc[slab, rows, :] = m_val[part]
                        lacc[slab, rows, :] = l_val[part]

        ngroups = d * nblk // ATTN_GROUP
        stage_scores(0, 0)
        stage_scores(1, 1)
        stage_softmax(0)

        def steady(t, carry):
            g = 2 * t + 2
            stage_scores(g, 0)
            stage_softmax(1)
            stage_output(g - 2, 0)
            stage_scores(g + 1, 1)
            stage_softmax(0)
            stage_output(g - 1, 1)
            return carry

        lax.fori_loop(0, (ngroups - 2) // 2, steady, 0, unroll=ATTN_UNROLL)
        stage_softmax(1)
        stage_output(ngroups - 2, 0)
        stage_output(ngroups - 1, 1)


def _attn_call(q, k, v):
    b, s, d = q.shape
    res, per = RESIDUES, s // RESIDUES
    spec = pl.BlockSpec((1, s, LANES), lambda bi, hp: (bi, 0, hp))
    return pl.pallas_call(
        functools.partial(_attn_kernel, seq=s),
        out_shape=jax.ShapeDtypeStruct(q.shape, F32),
        grid=(b, d // LANES),
        in_specs=[spec, spec, spec],
        out_specs=pl.BlockSpec((1, s, LANES), lambda bi, hp: (bi, 0, hp), pipeline_mode=pl.Buffered(1)),
        scratch_shapes=[
            pltpu.VMEM((res, per, LANES), F32),
            pltpu.VMEM((res, per, LANES), BF16),
            pltpu.VMEM((res, per, LANES), BF16),
            pltpu.VMEM((res, per, LANES), F32),
            pltpu.VMEM((res, per, LANES), F32),
            pltpu.VMEM((res, per, LANES), F32),
            pltpu.VMEM((3 * len(DILATIONS), QBLK, KWIN), F32),
            pltpu.VMEM((2, ATTN_GROUP, 2 * QBLK, KWIN), F32),
            pltpu.VMEM((2, ATTN_GROUP, 2 * QBLK, KWIN), BF16),
            pltpu.VMEM((2, ATTN_GROUP, QBLK, LANES), F32),
            pltpu.VMEM((QBLK * RESIDUES, LANES), F32),
        ],
        compiler_params=_params(("arbitrary", "arbitrary")),
        name="dilated_attn",
    )(q, k, v)


def _rope_tables(seq):
    pos = jnp.arange(seq, dtype=F32)
    inv_freq = ROPE_THETA ** (-jnp.arange(0, ROT_DIM, 2, dtype=F32) / ROT_DIM)
    ang = pos[:, None] * inv_freq[None, :]
    cos, sin = jnp.cos(ang), jnp.sin(ang)
    half = ROT_DIM // 2
    l64 = jnp.arange(LANES) % HEAD_DIM
    f = l64 % half
    cos_t = jnp.where(l64 < ROT_DIM, cos[:, f], 1.0)
    sin_t = jnp.where(l64 < half, -sin[:, f], jnp.where(l64 < ROT_DIM, sin[:, f], 0.0))
    return cos_t, sin_t


def kernel(x, mix_norm_g, mlp_norm_g, mlp_w1, mlp_w2, ab_w_in, a_spatial_w, a_spatial_b, a_vnorm_g, a_vnorm_b, b_conv_w, b_conv_b, b_norm_g, b_norm_b, ab_w_out, c_w_qkv, c_q_norm_g, c_k_norm_g, c_w_out):
    b, s, d = x.shape
    depth = mix_norm_g.shape[0]
    cos_t, sin_t = _rope_tables(s)
    w1s, w2s = mlp_w1.astype(BF16), mlp_w2.astype(BF16)
    wins, wouts = ab_w_in.astype(BF16), ab_w_out.astype(BF16)
    wqkvs, wos = c_w_qkv.astype(BF16), c_w_out.astype(BF16)
    for layer in range(depth):
        i = layer // 2
        g_mix = mix_norm_g[layer][None, :]
        g_mlp = mlp_norm_g[layer][None, :]
        if layer % 2 == 0:
            spw = a_spatial_w[i].astype(BF16).reshape(A_GROUPS // 2, 2 * CHUNK, CHUNK)
            spb = jnp.repeat(a_spatial_b[i].T, A_WIDTH // A_GROUPS, axis=1)
            x = _mixer_mlp_call(
                x.reshape(b * s, d), s, g_mix, wins, i, spw, spb,
                a_vnorm_g[i][None, :], a_vnorm_b[i][None, :], b_conv_w[i], b_conv_b[i][None, :],
                b_norm_g[i][None, :], b_norm_b[i][None, :], wouts, g_mlp, w1s, w2s, layer).reshape(b, s, d)
        else:
            qg = jnp.tile(c_q_norm_g[i], LANES // HEAD_DIM)[None, :]
            kg = jnp.tile(c_k_norm_g[i], LANES // HEAD_DIM)[None, :]
            q, k, v = _qkv_call(x.reshape(b * s, d), g_mix, wqkvs, i, qg, kg, cos_t, sin_t, s)
            a = _attn_call(q.reshape(b, s, d), k.reshape(b, s, d), v.reshape(b, s, d))
            x = _proj_mlp_call(x.reshape(b * s, d), a.reshape(b * s, d), wos, i, g_mlp, w1s, w2s,
                               layer).reshape(b, s, d)
    return x
```

```python
import functools

import jax
import jax.numpy as jnp
from jax import lax
from jax.experimental import pallas as pl
from jax.experimental.pallas import tpu as pltpu

F32 = jnp.float32
BF16 = jnp.bfloat16

D_MODEL = 1024
D_FF = 4 * D_MODEL
A_WIDTH = D_MODEL // 2
B_WIDTH = D_MODEL // 2
A_GROUPS = 8
CHUNK = 128
CONV_WIDTH = 31
HEAD_DIM = 64
ROT_DIM = HEAD_DIM // 4
ROPE_THETA = 500000.0
DILATIONS = (16, 4, 1)
HALF = 64
EPS = 1e-6
NEG = -1e30

LANES = 128
SUBLANES = 8
MXU_COLS = 256
HALO = 16
TM = 512
FF_CHUNK = 512
QBLK = 128
KWIN = 2 * QBLK
ATTN_GROUP = 1
ATTN_UNROLL = 31
RESIDUES = max(DILATIONS)
REGROUP_HOP = 4
VMEM_LIMIT = 56 * 1024 * 1024


def _params(sem, vmem=VMEM_LIMIT):
    return pltpu.CompilerParams(dimension_semantics=sem, vmem_limit_bytes=vmem)


def _rms(x, g):
    ms = jnp.mean(x * x, axis=-1, keepdims=True)
    return x * lax.rsqrt(ms + EPS) * g


def _layernorm(x, g, b):
    mu = jnp.mean(x, axis=-1, keepdims=True)
    xc = x - mu
    var = jnp.mean(xc * xc, axis=-1, keepdims=True)
    return xc * lax.rsqrt(var + EPS) * g + b


def _dot(a, b):
    return jnp.dot(a, b, preferred_element_type=F32)


def _run_interleaved(*parts):
    live = list(parts)
    while live:
        for part in list(live):
            if next(part, StopIteration) is StopIteration:
                live.remove(part)


def _layer_block(stack, index):
    zeros = (0,) * (stack.ndim - 1)
    return pl.BlockSpec((None,) + stack.shape[1:], lambda *_: (index,) + zeros, pipeline_mode=pl.Buffered(1))


def _mlp_steps(x, g, w1_ref, w2_ref):
    h = _rms(x, g).astype(BF16)
    acc = x
    for j in range(D_FF // FF_CHUNK):
        cols = slice(j * FF_CHUNK, (j + 1) * FF_CHUNK)
        a = _dot(h, w1_ref[:, cols])
        a = jnp.square(jnp.maximum(a, 0.0)).astype(BF16)
        acc = acc + _dot(a, w2_ref[cols, :])
        yield acc


def _mlp_body(x, g, w1_ref, w2_ref):
    for acc in _mlp_steps(x, g, w1_ref, w2_ref):
        pass
    return acc


def _proj_mlp_kernel(x_ref, a_ref, wo_ref, g_ref, w1_ref, w2_ref, o_ref):
    x = x_ref[...] + _dot(a_ref[...].astype(BF16), wo_ref[...])
    o_ref[...] = _mlp_body(x, g_ref[...], w1_ref, w2_ref)


def _proj_mlp_call(x2d, a2d, wo_stack, odd_index, g, w1_stack, w2_stack, layer):
    n = x2d.shape[0]
    const = lambda i: (0, 0)
    row = lambda i: (i, 0)
    return pl.pallas_call(
        _proj_mlp_kernel,
        out_shape=jax.ShapeDtypeStruct(x2d.shape, F32),
        grid=(n // TM,),
        in_specs=[
            pl.BlockSpec((TM, D_MODEL), row),
            pl.BlockSpec((TM, D_MODEL), row),
            _layer_block(wo_stack, odd_index),
            pl.BlockSpec((1, D_MODEL), const),
            _layer_block(w1_stack, layer),
            _layer_block(w2_stack, layer),
        ],
        out_specs=pl.BlockSpec((TM, D_MODEL), row),
        compiler_params=_params(("arbitrary",)),
        name="attn_out_proj_mlp",
    )(x2d, a2d, wo_stack, g, w1_stack, w2_stack)


def _mixer_mlp_kernel(xp_ref, x_ref, xn_ref, g_ref, win_ref, spw_ref, spb_ref, vg_ref, vb_ref,
                      cw_ref, cb_ref, cng_ref, cnb_ref, wout_ref, g2_ref, w1_ref, w2_ref,
                      o_ref, gs_ref, sh_ref, xm_ref, *, blocks_per_seq):
    t = pl.program_id(0)
    nsteps = pl.num_programs(0)

    def mlp():
        for acc in _mlp_steps(xm_ref[...], g2_ref[...], w1_ref, w2_ref):
            yield
        o_ref[...] = acc
        yield

    def mixer():
        i = t % blocks_per_seq
        last = blocks_per_seq - 1
        g = g_ref[...]
        x = x_ref[...]
        h = _rms(x, g).astype(BF16)
        h_all = jnp.concatenate(
            [_rms(xp_ref[...], g).astype(BF16), h, _rms(xn_ref[...], g).astype(BF16)], axis=0)

        zb = _dot(h_all, win_ref[:, 2 * A_WIDTH:])
        glu = zb[:, :B_WIDTH] * jax.nn.sigmoid(zb[:, B_WIDTH:])
        row = lax.broadcasted_iota(jnp.int32, glu.shape, 0)
        inside = ((row >= HALO) | (i > 0)) & ((row < TM + HALO) | (i < last))
        gs_ref[...] = jnp.where(inside, glu, 0.0)
        yield
        conv = jnp.broadcast_to(cb_ref[...], (TM, B_WIDTH))
        first_tap = HALO - CONV_WIDTH // 2
        for shift in range(SUBLANES):
            src = gs_ref
            if shift:
                sh_ref[...] = gs_ref[pl.ds(shift, TM + 2 * HALO - SUBLANES), :]
                src = sh_ref
            for base in range(0, 2 * HALO, SUBLANES):
                k = base + shift - first_tap
                if 0 <= k < CONV_WIDTH:
                    conv = conv + cw_ref[k:k + 1, :] * src[pl.ds(base, TM), :]
            if shift % 2 == 1 and shift < SUBLANES - 1:
                yield
        yb = _layernorm(conv, cng_ref[...], cnb_ref[...])
        yb = yb * jax.nn.sigmoid(yb)
        yield

        za = jax.nn.gelu(_dot(h, win_ref[:, :2 * A_WIDTH]))
        u = za[:, :A_WIDTH]
        v = _layernorm(za[:, A_WIDTH:], vg_ref[...], vb_ref[...]).astype(BF16)
        yield
        lane = lax.broadcasted_iota(jnp.int32, (CHUNK, LANES), 1)
        first_group = lane < LANES // 2
        ya_chunks = []
        for c in range(TM // CHUNK):
            vc = v[c * CHUNK:(c + 1) * CHUNK, :]
            parts = []
            for j in range(A_WIDTH // LANES):
                r = _dot(spw_ref[j], vc[:, j * LANES:(j + 1) * LANES])
                parts.append(jnp.where(first_group, r[:CHUNK], r[CHUNK:]))
            sv = jnp.concatenate(parts, axis=1) + spb_ref[...]
            ya_chunks.append(u[c * CHUNK:(c + 1) * CHUNK, :] * sv)
            if c % 2 == 1:
                yield
        ya = jnp.concatenate(ya_chunks, axis=0)

        out = _dot(ya.astype(BF16), wout_ref[:A_WIDTH, :]) + _dot(yb.astype(BF16), wout_ref[A_WIDTH:, :])
        xm_ref[...] = x + out
        yield

    @pl.when(t == 0)
    def _():
        _run_interleaved(mixer())

    @pl.when((t > 0) & (t < nsteps - 1))
    def _():
        _run_interleaved(mlp(), mixer())

    @pl.when(t == nsteps - 1)
    def _():
        _run_interleaved(mlp())


def _mixer_mlp_call(x2d, seq, g, win_stack, even_index, spw, spb, vg, vb, cw, cb, cng, cnb, wout_stack,
                    g2, w1_stack, w2_stack, layer):
    n, d = x2d.shape
    nblk = n // TM
    per = TM // HALO
    const2 = lambda t: (0, 0)
    const3 = lambda t: (0, 0, 0)
    cur = lambda t: jnp.minimum(t, nblk - 1)
    resident = dict(pipeline_mode=pl.Buffered(1))
    return pl.pallas_call(
        functools.partial(_mixer_mlp_kernel, blocks_per_seq=seq // TM),
        out_shape=jax.ShapeDtypeStruct(x2d.shape, F32),
        grid=(nblk + 1,),
        in_specs=[
            pl.BlockSpec((HALO, d), lambda t: (jnp.maximum(cur(t) * per - 1, 0), 0)),
            pl.BlockSpec((TM, d), lambda t: (cur(t), 0)),
            pl.BlockSpec((HALO, d), lambda t: (jnp.minimum((cur(t) + 1) * per, n // HALO - 1), 0)),
            pl.BlockSpec((1, d), const2),
            _layer_block(win_stack, even_index),
            pl.BlockSpec(spw.shape, const3, **resident),
            pl.BlockSpec(spb.shape, const2, **resident),
            pl.BlockSpec((1, A_WIDTH), const2),
            pl.BlockSpec((1, A_WIDTH), const2),
            pl.BlockSpec(cw.shape, const2),
            pl.BlockSpec((1, B_WIDTH), const2),
            pl.BlockSpec((1, B_WIDTH), const2),
            pl.BlockSpec((1, B_WIDTH), const2),
            _layer_block(wout_stack, even_index),
            pl.BlockSpec((1, d), const2),
            _layer_block(w1_stack, layer),
            _layer_block(w2_stack, layer),
        ],
        out_specs=pl.BlockSpec((TM, d), lambda t: (jnp.maximum(t - 1, 0), 0)),
        scratch_shapes=[pltpu.VMEM((TM + 2 * HALO, B_WIDTH), F32),
                        pltpu.VMEM((TM + 2 * HALO - SUBLANES, B_WIDTH), F32),
                        pltpu.VMEM((TM, D_MODEL), F32)],
        compiler_params=_params(("arbitrary",)),
        name="mixer_ab_mlp",
    )(x2d, x2d, x2d, g, win_stack, spw, spb, vg, vb, cw, cb, cng, cnb, wout_stack, g2, w1_stack, w2_stack)


def _qkv_kernel(x_ref, g_ref, w_ref, qg_ref, kg_ref, cos_ref, sin_ref, q_ref, k_ref, v_ref, raw_ref):
    i = pl.program_id(0)
    nsteps = pl.num_programs(0)

    def project():
        h = _rms(x_ref[...], g_ref[...]).astype(BF16)
        for part in reversed(range(3)):
            cols = slice(part * D_MODEL, (part + 1) * D_MODEL)
            raw_ref[:, cols] = _dot(h, w_ref[:, cols])
            yield

    def finish():
        lane = lax.broadcasted_iota(jnp.int32, (TM, LANES), 1)
        low = (lane % HEAD_DIM) < ROT_DIM // 2
        cos = cos_ref[...]
        sin = sin_ref[...]
        hr = lax.broadcasted_iota(jnp.int32, (MXU_COLS, MXU_COLS), 0) // HEAD_DIM
        hc = lax.broadcasted_iota(jnp.int32, (MXU_COLS, MXU_COLS), 1) // HEAD_DIM
        same_head = jnp.where(hr == hc, 1.0, 0.0).astype(BF16)

        def head_norm_rope(t4, gain):
            ms = _dot((t4 * t4).astype(BF16), same_head) * (1.0 / HEAD_DIM)
            outs = []
            for half in range(MXU_COLS // LANES):
                sl = slice(half * LANES, (half + 1) * LANES)
                t = t4[:, sl] * lax.rsqrt(ms[:, sl] + EPS) * gain
                partner = jnp.where(low, pltpu.roll(t, LANES - ROT_DIM // 2, 1), pltpu.roll(t, ROT_DIM // 2, 1))
                outs.append(t * cos + partner * sin)
            return jnp.concatenate(outs, axis=1)

        v_ref[...] = raw_ref[:, 2 * D_MODEL:]
        yield
        for j in range(D_MODEL // MXU_COLS):
            cols = slice(j * MXU_COLS, (j + 1) * MXU_COLS)
            kcols = slice(D_MODEL + j * MXU_COLS, D_MODEL + (j + 1) * MXU_COLS)
            k_ref[:, cols] = head_norm_rope(raw_ref[:, kcols], kg_ref[...])
        yield
        for j in range(D_MODEL // MXU_COLS):
            cols = slice(j * MXU_COLS, (j + 1) * MXU_COLS)
            q_ref[:, cols] = head_norm_rope(raw_ref[:, cols], qg_ref[...]) * (HEAD_DIM ** -0.5)
        yield

    @pl.when(i == 0)
    def _():
        _run_interleaved(project())

    @pl.when((i > 0) & (i < nsteps - 1))
    def _():
        _run_interleaved(finish(), project())

    @pl.when(i == nsteps - 1)
    def _():
        _run_interleaved(finish())


def _qkv_call(x2d, g, w_stack, odd_index, qg, kg, cos_t, sin_t, seq):
    n = x2d.shape[0]
    nblk = n // TM
    pos_blocks = seq // TM
    const = lambda i: (0, 0)
    lagged = lambda i: (jnp.maximum(i - 1, 0), 0)
    lagged_pos = lambda i: (jnp.maximum(i - 1, 0) % pos_blocks, 0)
    out = jax.ShapeDtypeStruct(x2d.shape, F32)
    return pl.pallas_call(
        _qkv_kernel,
        out_shape=(out, out, out),
        grid=(nblk + 1,),
        in_specs=[
            pl.BlockSpec((TM, D_MODEL), lambda i: (jnp.minimum(i, nblk - 1), 0)),
            pl.BlockSpec((1, D_MODEL), const),
            _layer_block(w_stack, odd_index),
            pl.BlockSpec((1, LANES), const),
            pl.BlockSpec((1, LANES), const),
            pl.BlockSpec((TM, LANES), lagged_pos),
            pl.BlockSpec((TM, LANES), lagged_pos),
        ],
        out_specs=(pl.BlockSpec((TM, D_MODEL), lagged),) * 3,
        scratch_shapes=[pltpu.VMEM((TM, 3 * D_MODEL), F32)],
        compiler_params=_params(("arbitrary",)),
        name="qkv",
    )(x2d, g, w_stack, qg, kg, cos_t, sin_t)


def _attn_kernel(qn_ref, kn_ref, vn_ref, o_ref, q_ref, k_ref, v_ref, oacc, macc, lacc, bias_ref,
                 s_scr, p_scr, m_scr, stage, *, seq):
    per = seq // RESIDUES

    def regroup(c, carry):
        hop = REGROUP_HOP
        span = QBLK * RESIDUES
        seg = span // hop
        for src_ref, dst_ref in ((qn_ref, q_ref), (kn_ref, k_ref), (vn_ref, v_ref)):
            for r1 in range(hop):
                for part in range(seg // QBLK):
                    rows = pl.ds(c * span + r1 + part * QBLK * hop, QBLK, stride=hop)
                    stage[pl.ds(r1 * seg + part * QBLK, QBLK), :] = src_ref[0, rows, :]
            for r in range(RESIDUES):
                r1, r2 = r % hop, r // hop
                val = stage[pl.ds(r1 * seg + r2, QBLK, stride=RESIDUES // hop), :]
                dst_ref[r, pl.ds(pl.multiple_of(c * QBLK, QBLK), QBLK), :] = val.astype(dst_ref.dtype)
        return carry

    lax.fori_loop(0, per // QBLK, regroup, 0)
    lane = lax.broadcasted_iota(jnp.int32, (QBLK, LANES), 1)
    head0 = lane < HEAD_DIM
    ones_keys = jnp.ones((KWIN, LANES), BF16)
    row = lax.broadcasted_iota(jnp.int32, (QBLK, KWIN), 0)
    col = lax.broadcasted_iota(jnp.int32, (QBLK, KWIN), 1)
    for pat, d in enumerate(DILATIONS):
        nslab = RESIDUES // d
        qp, kp = QBLK // nslab, KWIN // nslab
        qpos = nslab * (row % qp) + row // qp
        kpos = col if d == 1 else nslab * (col % kp) + col // kp
        for which, shift in enumerate((HALF, 0, 2 * HALF)):
            diff = kpos - qpos - shift
            bias_ref[3 * pat + which] = jnp.where((diff >= -HALF) & (diff <= HALF), 0.0, NEG)

    for pat, d in enumerate(DILATIONS):
        nslab = RESIDUES // d
        qp, kp = QBLK // nslab, KWIN // nslab
        nblk = seq // d // QBLK

        def locate(idx, nblk=nblk):
            return idx // nblk, idx % nblk

        def slab_rows(seg, start, size, d=d, nslab=nslab):
            return [(seg + d * j, pl.ds(start, size)) for j in range(nslab)]

        def gather(ref, pieces):
            parts = [ref[slab, rows, :] for slab, rows in pieces]
            return parts[0] if len(parts) == 1 else jnp.concatenate(parts, axis=0)

        def query_pieces(seg, n, qp=qp):
            return slab_rows(seg, pl.multiple_of(n * qp, qp), qp)

        def key_block(ref, ref_n, seg, n, d=d, qp=qp, kp=kp):
            if d == 1:
                start = jnp.clip(n * QBLK - HALF, 0, seq - KWIN)
                return ref_n[0, pl.ds(pl.multiple_of(start, HALF), KWIN), :].astype(BF16)
            start = jnp.clip(n * qp - kp // 4, 0, per - kp)
            return gather(ref, slab_rows(seg, pl.multiple_of(start, kp // 4), kp))

        def stage_scores(g, slot, pat=pat, nblk=nblk):
            for u in range(ATTN_GROUP):
                seg, n = locate(g * ATTN_GROUP + u)
                qb = gather(q_ref, query_pieces(seg, n)).astype(BF16)
                kb = key_block(k_ref, kn_ref, seg, n)
                bias = bias_ref[3 * pat + jnp.where(n == 0, 1, jnp.where(n == nblk - 1, 2, 0))]
                zero = jnp.zeros_like(qb)
                qq = jnp.concatenate([jnp.where(head0, qb, zero), jnp.where(head0, zero, qb)], axis=0)
                s = lax.dot_general(qq, kb, (((1,), (1,)), ((), ())), preferred_element_type=F32)
                s_scr[slot, u, :QBLK, :] = s[:QBLK] + bias
                s_scr[slot, u, QBLK:, :] = s[QBLK:] + bias

        def stage_softmax(slot):
            for u in range(ATTN_GROUP):
                s = s_scr[slot, u]
                m = jnp.max(s, axis=-1, keepdims=True)
                p_scr[slot, u] = jnp.exp(s - m).astype(BF16)
                m_scr[slot, u] = jnp.where(head0, m[:QBLK], m[QBLK:])

        def stage_output(g, slot, d=d, qp=qp, nslab=nslab):
            results = []
            for u in range(ATTN_GROUP):
                seg, n = locate(g * ATTN_GROUP + u)
                v_ones = jnp.concatenate([key_block(v_ref, vn_ref, seg, n), ones_keys], axis=1)
                pv = _dot(p_scr[slot, u], v_ones)
                acc_new = jnp.where(head0, pv[:QBLK, :LANES], pv[QBLK:, :LANES])
                l_new = jnp.where(head0, pv[:QBLK, LANES:], pv[QBLK:, LANES:])
                m_new = m_scr[slot, u]
                pieces = query_pieces(seg, n)
                if d == DILATIONS[0]:
                    results.append((pieces, acc_new, m_new, l_new))
                    continue
                m_old = jnp.concatenate([macc[slab, rows, :] for slab, rows in pieces], axis=0)
                acc_old = jnp.concatenate([oacc[slab, rows, :] for slab, rows in pieces], axis=0)
                l_old = jnp.concatenate([lacc[slab, rows, :] for slab, rows in pieces], axis=0)
                m_tot = jnp.maximum(m_old, m_new)
                w_old = jnp.exp(m_old - m_tot)
                w_new = jnp.exp(m_new - m_tot)
                acc_tot = acc_old * w_old + acc_new * w_new
                l_tot = l_old * w_old + l_new * w_new
                if d == DILATIONS[-1]:
                    results.append((pieces, acc_tot / l_tot, None, None))
                else:
                    results.append((pieces, acc_tot, m_tot, l_tot))
            for (pieces, acc_val, m_val, l_val), u in zip(results, range(ATTN_GROUP)):
                for k, (slab, rows) in enumerate(pieces):
                    part = slice(k * qp, (k + 1) * qp)
                    if d == DILATIONS[-1]:
                        n = (g * ATTN_GROUP + u) % (seq // QBLK)
                        o_ref[0, pl.ds(n * QBLK + k, qp, stride=nslab), :] = acc_val[part]
                    else:
                        oacc[slab, rows, :] = acc_val[part]
                        macc[slab, rows, :] = m_val[part]
                        lacc[slab, rows, :] = l_val[part]

        ngroups = d * nblk // ATTN_GROUP
        stage_scores(0, 0)
        stage_scores(1, 1)
        stage_softmax(0)

        def steady(t, carry):
            g = 2 * t + 2
            stage_scores(g, 0)
            stage_softmax(1)
            stage_output(g - 2, 0)
            stage_scores(g + 1, 1)
            stage_softmax(0)
            stage_output(g - 1, 1)
            return carry

        lax.fori_loop(0, (ngroups - 2) // 2, steady, 0, unroll=ATTN_UNROLL)
        stage_softmax(1)
        stage_output(ngroups - 2, 0)
        stage_output(ngroups - 1, 1)


def _attn_call(q, k, v):
    b, s, d = q.shape
    res, per = RESIDUES, s // RESIDUES
    spec = pl.BlockSpec((1, s, LANES), lambda bi, hp: (bi, 0, hp))
    return pl.pallas_call(
        functools.partial(_attn_kernel, seq=s),
        out_shape=jax.ShapeDtypeStruct(q.shape, F32),
        grid=(b, d // LANES),
        in_specs=[spec, spec, spec],
        out_specs=pl.BlockSpec((1, s, LANES), lambda bi, hp: (bi, 0, hp), pipeline_mode=pl.Buffered(1)),
        scratch_shapes=[
            pltpu.VMEM((res, per, LANES), F32),
            pltpu.VMEM((res, per, LANES), BF16),
            pltpu.VMEM((res, per, LANES), BF16),
            pltpu.VMEM((res, per, LANES), F32),
            pltpu.VMEM((res, per, LANES), F32),
            pltpu.VMEM((res, per, LANES), F32),
            pltpu.VMEM((3 * len(DILATIONS), QBLK, KWIN), F32),
            pltpu.VMEM((2, ATTN_GROUP, 2 * QBLK, KWIN), F32),
            pltpu.VMEM((2, ATTN_GROUP, 2 * QBLK, KWIN), BF16),
            pltpu.VMEM((2, ATTN_GROUP, QBLK, LANES), F32),
            pltpu.VMEM((QBLK * RESIDUES, LANES), F32),
        ],
        compiler_params=_params(("arbitrary", "arbitrary")),
        name="dilated_attn",
    )(q, k, v)


def _rope_tables(seq):
    pos = jnp.arange(seq, dtype=F32)
    inv_freq = ROPE_THETA ** (-jnp.arange(0, ROT_DIM, 2, dtype=F32) / ROT_DIM)
    ang = pos[:, None] * inv_freq[None, :]
    cos, sin = jnp.cos(ang), jnp.sin(ang)
    half = ROT_DIM // 2
    l64 = jnp.arange(LANES) % HEAD_DIM
    f = l64 % half
    cos_t = jnp.where(l64 < ROT_DIM, cos[:, f], 1.0)
    sin_t = jnp.where(l64 < half, -sin[:, f], jnp.where(l64 < ROT_DIM, sin[:, f], 0.0))
    return cos_t, sin_t


def kernel(x, mix_norm_g, mlp_norm_g, mlp_w1, mlp_w2, ab_w_in, a_spatial_w, a_spatial_b, a_vnorm_g, a_vnorm_b, b_conv_w, b_conv_b, b_norm_g, b_norm_b, ab_w_out, c_w_qkv, c_q_norm_g, c_k_norm_g, c_w_out):
    b, s, d = x.shape
    depth = mix_norm_g.shape[0]
    cos_t, sin_t = _rope_tables(s)
    w1s, w2s = mlp_w1.astype(BF16), mlp_w2.astype(BF16)
    wins, wouts = ab_w_in.astype(BF16), ab_w_out.astype(BF16)
    wqkvs, wos = c_w_qkv.astype(BF16), c_w_out.astype(BF16)
    for layer in range(depth):
        i = layer // 2
        g_mix = mix_norm_g[layer][None, :]
        g_mlp = mlp_norm_g[layer][None, :]
        if layer % 2 == 0:
            spw = a_spatial_w[i].astype(BF16).reshape(A_GROUPS // 2, 2 * CHUNK, CHUNK)
            spb = jnp.repeat(a_spatial_b[i].T, A_WIDTH // A_GROUPS, axis=1)
            x = _mixer_mlp_call(
                x.reshape(b * s, d), s, g_mix, wins, i, spw, spb,
                a_vnorm_g[i][None, :], a_vnorm_b[i][None, :], b_conv_w[i], b_conv_b[i][None, :],
                b_norm_g[i][None, :], b_norm_b[i][None, :], wouts, g_mlp, w1s, w2s, layer).reshape(b, s, d)
        else:
            qg = jnp.tile(c_q_norm_g[i], LANES // HEAD_DIM)[None, :]
            kg = jnp.tile(c_k_norm_g[i], LANES // HEAD_DIM)[None, :]
            q, k, v = _qkv_call(x.reshape(b * s, d), g_mix, wqkvs, i, qg, kg, cos_t, sin_t, s)
            a = _attn_call(q.reshape(b, s, d), k.reshape(b, s, d), v.reshape(b, s, d))
            x = _proj_mlp_call(x.reshape(b * s, d), a.reshape(b * s, d), wos, i, g_mlp, w1s, w2s,
                               layer).reshape(b, s, d)
    return x
```

```python
import functools

import jax
import jax.numpy as jnp
from jax import lax
from jax.experimental import pallas as pl
from jax.experimental.pallas import tpu as pltpu

F32 = jnp.float32
BF16 = jnp.bfloat16

D_MODEL = 1024
D_FF = 4 * D_MODEL
A_WIDTH = D_MODEL // 2
B_WIDTH = D_MODEL // 2
A_GROUPS = 8
CHUNK = 128
CONV_WIDTH = 31
HEAD_DIM = 64
ROT_DIM = HEAD_DIM // 4
ROPE_THETA = 500000.0
DILATIONS = (16, 4, 1)
HALF = 64
EPS = 1e-6
NEG = -1e30

LANES = 128
SUBLANES = 8
MXU_COLS = 256
HALO = 16
TM = 512
FF_CHUNK = 512
QBLK = 128
KWIN = 2 * QBLK
ATTN_GROUP = 1
ATTN_UNROLL = 31
RESIDUES = max(DILATIONS)
REGROUP_HOP = 4
VMEM_LIMIT = 56 * 1024 * 1024
ATTN_VMEM_LIMIT = 60 * 1024 * 1024


def _params(sem, vmem=VMEM_LIMIT):
    return pltpu.CompilerParams(dimension_semantics=sem, vmem_limit_bytes=vmem)


def _rms(x, g):
    ms = jnp.mean(x * x, axis=-1, keepdims=True)
    return x * lax.rsqrt(ms + EPS) * g


def _layernorm(x, g, b):
    mu = jnp.mean(x, axis=-1, keepdims=True)
    xc = x - mu
    var = jnp.mean(xc * xc, axis=-1, keepdims=True)
    return xc * lax.rsqrt(var + EPS) * g + b


def _dot(a, b):
    return jnp.dot(a, b, preferred_element_type=F32)


def _run_interleaved(*parts):
    live = list(parts)
    while live:
        for part in list(live):
            if next(part, StopIteration) is StopIteration:
                live.remove(part)


def _layer_block(stack, index):
    zeros = (0,) * (stack.ndim - 1)
    return pl.BlockSpec((None,) + stack.shape[1:], lambda *_: (index,) + zeros, pipeline_mode=pl.Buffered(1))


def _mlp_steps(x, g, w1_ref, w2_ref):
    h = _rms(x, g).astype(BF16)
    acc = x
    for j in range(D_FF // FF_CHUNK):
        cols = slice(j * FF_CHUNK, (j + 1) * FF_CHUNK)
        a = _dot(h, w1_ref[:, cols])
        a = jnp.square(jnp.maximum(a, 0.0)).astype(BF16)
        acc = acc + _dot(a, w2_ref[cols, :])
        yield acc


def _mlp_body(x, g, w1_ref, w2_ref):
    for acc in _mlp_steps(x, g, w1_ref, w2_ref):
        pass
    return acc


def _proj_mlp_kernel(x_ref, a_ref, wo_ref, g_ref, w1_ref, w2_ref, o_ref):
    x = x_ref[...] + _dot(a_ref[...].astype(BF16), wo_ref[...])
    o_ref[...] = _mlp_body(x, g_ref[...], w1_ref, w2_ref)


def _proj_mlp_call(x2d, a2d, wo_stack, odd_index, g, w1_stack, w2_stack, layer):
    n = x2d.shape[0]
    const = lambda i: (0, 0)
    row = lambda i: (i, 0)
    return pl.pallas_call(
        _proj_mlp_kernel,
        out_shape=jax.ShapeDtypeStruct(x2d.shape, F32),
        grid=(n // TM,),
        in_specs=[
            pl.BlockSpec((TM, D_MODEL), row),
            pl.BlockSpec((TM, D_MODEL), row),
            _layer_block(wo_stack, odd_index),
            pl.BlockSpec((1, D_MODEL), const),
            _layer_block(w1_stack, layer),
            _layer_block(w2_stack, layer),
        ],
        out_specs=pl.BlockSpec((TM, D_MODEL), row),
        compiler_params=_params(("arbitrary",)),
        name="attn_out_proj_mlp",
    )(x2d, a2d, wo_stack, g, w1_stack, w2_stack)


def _mixer_mlp_kernel(xp_ref, x_ref, xn_ref, g_ref, win_ref, spw_ref, spb_ref, vg_ref, vb_ref,
                      cw_ref, cb_ref, cng_ref, cnb_ref, wout_ref, g2_ref, w1_ref, w2_ref,
                      o_ref, gs_ref, sh_ref, xm_ref, *, blocks_per_seq):
    t = pl.program_id(0)
    nsteps = pl.num_programs(0)

    def mlp():
        for acc in _mlp_steps(xm_ref[...], g2_ref[...], w1_ref, w2_ref):
            yield
        o_ref[...] = acc
        yield

    def mixer():
        i = t % blocks_per_seq
        last = blocks_per_seq - 1
        g = g_ref[...]
        x = x_ref[...]
        h = _rms(x, g).astype(BF16)
        h_all = jnp.concatenate(
            [_rms(xp_ref[...], g).astype(BF16), h, _rms(xn_ref[...], g).astype(BF16)], axis=0)

        zb = _dot(h_all, win_ref[:, 2 * A_WIDTH:])
        glu = zb[:, :B_WIDTH] * jax.nn.sigmoid(zb[:, B_WIDTH:])
        row = lax.broadcasted_iota(jnp.int32, glu.shape, 0)
        inside = ((row >= HALO) | (i > 0)) & ((row < TM + HALO) | (i < last))
        gs_ref[...] = jnp.where(inside, glu, 0.0)
        yield
        conv = jnp.broadcast_to(cb_ref[...], (TM, B_WIDTH))
        first_tap = HALO - CONV_WIDTH // 2
        for shift in range(SUBLANES):
            src = gs_ref
            if shift:
                sh_ref[...] = gs_ref[pl.ds(shift, TM + 2 * HALO - SUBLANES), :]
                src = sh_ref
            for base in range(0, 2 * HALO, SUBLANES):
                k = base + shift - first_tap
                if 0 <= k < CONV_WIDTH:
                    conv = conv + cw_ref[k:k + 1, :] * src[pl.ds(base, TM), :]
            if shift % 2 == 1 and shift < SUBLANES - 1:
                yield
        yb = _layernorm(conv, cng_ref[...], cnb_ref[...])
        yb = yb * jax.nn.sigmoid(yb)
        yield

        za = jax.nn.gelu(_dot(h, win_ref[:, :2 * A_WIDTH]))
        u = za[:, :A_WIDTH]
        v = _layernorm(za[:, A_WIDTH:], vg_ref[...], vb_ref[...]).astype(BF16)
        yield
        lane = lax.broadcasted_iota(jnp.int32, (CHUNK, LANES), 1)
        first_group = lane < LANES // 2
        ya_chunks = []
        for c in range(TM // CHUNK):
            vc = v[c * CHUNK:(c + 1) * CHUNK, :]
            parts = []
            for j in range(A_WIDTH // LANES):
                r = _dot(spw_ref[j], vc[:, j * LANES:(j + 1) * LANES])
                parts.append(jnp.where(first_group, r[:CHUNK], r[CHUNK:]))
            sv = jnp.concatenate(parts, axis=1) + spb_ref[...]
            ya_chunks.append(u[c * CHUNK:(c + 1) * CHUNK, :] * sv)
            if c % 2 == 1:
                yield
        ya = jnp.concatenate(ya_chunks, axis=0)

        out = _dot(ya.astype(BF16), wout_ref[:A_WIDTH, :]) + _dot(yb.astype(BF16), wout_ref[A_WIDTH:, :])
        xm_ref[...] = x + out
        yield

    @pl.when(t == 0)
    def _():
        _run_interleaved(mixer())

    @pl.when((t > 0) & (t < nsteps - 1))
    def _():
        _run_interleaved(mlp(), mixer())

    @pl.when(t == nsteps - 1)
    def _():
        _run_interleaved(mlp())


def _mixer_mlp_call(x2d, seq, g, win_stack, even_index, spw, spb, vg, vb, cw, cb, cng, cnb, wout_stack,
                    g2, w1_stack, w2_stack, layer):
    n, d = x2d.shape
    nblk = n // TM
    per = TM // HALO
    const2 = lambda t: (0, 0)
    const3 = lambda t: (0, 0, 0)
    cur = lambda t: jnp.minimum(t, nblk - 1)
    resident = dict(pipeline_mode=pl.Buffered(1))
    return pl.pallas_call(
        functools.partial(_mixer_mlp_kernel, blocks_per_seq=seq // TM),
        out_shape=jax.ShapeDtypeStruct(x2d.shape, F32),
        grid=(nblk + 1,),
        in_specs=[
            pl.BlockSpec((HALO, d), lambda t: (jnp.maximum(cur(t) * per - 1, 0), 0)),
            pl.BlockSpec((TM, d), lambda t: (cur(t), 0)),
            pl.BlockSpec((HALO, d), lambda t: (jnp.minimum((cur(t) + 1) * per, n // HALO - 1), 0)),
            pl.BlockSpec((1, d), const2),
            _layer_block(win_stack, even_index),
            pl.BlockSpec(spw.shape, const3, **resident),
            pl.BlockSpec(spb.shape, const2, **resident),
            pl.BlockSpec((1, A_WIDTH), const2),
            pl.BlockSpec((1, A_WIDTH), const2),
            pl.BlockSpec(cw.shape, const2),
            pl.BlockSpec((1, B_WIDTH), const2),
            pl.BlockSpec((1, B_WIDTH), const2),
            pl.BlockSpec((1, B_WIDTH), const2),
            _layer_block(wout_stack, even_index),
            pl.BlockSpec((1, d), const2),
            _layer_block(w1_stack, layer),
            _layer_block(w2_stack, layer),
        ],
        out_specs=pl.BlockSpec((TM, d), lambda t: (jnp.maximum(t - 1, 0), 0)),
        scratch_shapes=[pltpu.VMEM((TM + 2 * HALO, B_WIDTH), F32),
                        pltpu.VMEM((TM + 2 * HALO - SUBLANES, B_WIDTH), F32),
                        pltpu.VMEM((TM, D_MODEL), F32)],
        compiler_params=_params(("arbitrary",)),
        name="mixer_ab_mlp",
    )(x2d, x2d, x2d, g, win_stack, spw, spb, vg, vb, cw, cb, cng, cnb, wout_stack, g2, w1_stack, w2_stack)


def _qkv_kernel(x_ref, g_ref, w_ref, qg_ref, kg_ref, cos_ref, sin_ref, q_ref, k_ref, v_ref, raw_ref):
    i = pl.program_id(0)
    nsteps = pl.num_programs(0)

    def project():
        h = _rms(x_ref[...], g_ref[...]).astype(BF16)
        for part in reversed(range(3)):
            cols = slice(part * D_MODEL, (part + 1) * D_MODEL)
            raw_ref[:, cols] = _dot(h, w_ref[:, cols])
            yield

    def finish():
        lane = lax.broadcasted_iota(jnp.int32, (TM, LANES), 1)
        low = (lane % HEAD_DIM) < ROT_DIM // 2
        cos = cos_ref[...]
        sin = sin_ref[...]
        hr = lax.broadcasted_iota(jnp.int32, (MXU_COLS, MXU_COLS), 0) // HEAD_DIM
        hc = lax.broadcasted_iota(jnp.int32, (MXU_COLS, MXU_COLS), 1) // HEAD_DIM
        same_head = jnp.where(hr == hc, 1.0, 0.0).astype(BF16)

        def head_norm_rope(t4, gain):
            ms = _dot((t4 * t4).astype(BF16), same_head) * (1.0 / HEAD_DIM)
            outs = []
            for half in range(MXU_COLS // LANES):
                sl = slice(half * LANES, (half + 1) * LANES)
                t = t4[:, sl] * lax.rsqrt(ms[:, sl] + EPS) * gain
                partner = jnp.where(low, pltpu.roll(t, LANES - ROT_DIM // 2, 1), pltpu.roll(t, ROT_DIM // 2, 1))
                outs.append(t * cos + partner * sin)
            return jnp.concatenate(outs, axis=1)

        v_ref[...] = raw_ref[:, 2 * D_MODEL:]
        yield
        for j in range(D_MODEL // MXU_COLS):
            cols = slice(j * MXU_COLS, (j + 1) * MXU_COLS)
            kcols = slice(D_MODEL + j * MXU_COLS, D_MODEL + (j + 1) * MXU_COLS)
            k_ref[:, cols] = head_norm_rope(raw_ref[:, kcols], kg_ref[...])
        yield
        for j in range(D_MODEL // MXU_COLS):
            cols = slice(j * MXU_COLS, (j + 1) * MXU_COLS)
            q_ref[:, cols] = head_norm_rope(raw_ref[:, cols], qg_ref[...]) * (HEAD_DIM ** -0.5)
        yield

    @pl.when(i == 0)
    def _():
        _run_interleaved(project())

    @pl.when((i > 0) & (i < nsteps - 1))
    def _():
        _run_interleaved(finish(), project())

    @pl.when(i == nsteps - 1)
    def _():
        _run_interleaved(finish())


def _qkv_call(x2d, g, w_stack, odd_index, qg, kg, cos_t, sin_t, seq):
    n = x2d.shape[0]
    nblk = n // TM
    pos_blocks = seq // TM
    const = lambda i: (0, 0)
    lagged = lambda i: (jnp.maximum(i - 1, 0), 0)
    lagged_pos = lambda i: (jnp.maximum(i - 1, 0) % pos_blocks, 0)
    out = jax.ShapeDtypeStruct(x2d.shape, F32)
    return pl.pallas_call(
        _qkv_kernel,
        out_shape=(out, out, out),
        grid=(nblk + 1,),
        in_specs=[
            pl.BlockSpec((TM, D_MODEL), lambda i: (jnp.minimum(i, nblk - 1), 0)),
            pl.BlockSpec((1, D_MODEL), const),
            _layer_block(w_stack, odd_index),
            pl.BlockSpec((1, LANES), const),
            pl.BlockSpec((1, LANES), const),
            pl.BlockSpec((TM, LANES), lagged_pos),
            pl.BlockSpec((TM, LANES), lagged_pos),
        ],
        out_specs=(pl.BlockSpec((TM, D_MODEL), lagged),) * 3,
        scratch_shapes=[pltpu.VMEM((TM, 3 * D_MODEL), F32)],
        compiler_params=_params(("arbitrary",)),
        name="qkv",
    )(x2d, g, w_stack, qg, kg, cos_t, sin_t)


def _attn_kernel(qn_ref, kn_ref, vn_ref, o_ref, q_ref, k_ref, v_ref, oacc, macc, lacc, bias_ref,
                 s_scr, p_scr, m_scr, stage, kb_ref, vb_ref, *, seq):
    per = seq // RESIDUES

    def regroup(c, carry):
        hop = REGROUP_HOP
        span = QBLK * RESIDUES
        seg = span // hop
        rows_c = pl.ds(pl.multiple_of(c * span, span), span)
        kb_ref[rows_c, :] = kn_ref[0, rows_c, :].astype(BF16)
        vb_ref[rows_c, :] = vn_ref[0, rows_c, :].astype(BF16)
        for src_ref, dst_ref in ((qn_ref, q_ref), (kn_ref, k_ref), (vn_ref, v_ref)):
            for r1 in range(hop):
                for part in range(seg // QBLK):
                    rows = pl.ds(c * span + r1 + part * QBLK * hop, QBLK, stride=hop)
                    stage[pl.ds(r1 * seg + part * QBLK, QBLK), :] = src_ref[0, rows, :]
            for r in range(RESIDUES):
                r1, r2 = r % hop, r // hop
                val = stage[pl.ds(r1 * seg + r2, QBLK, stride=RESIDUES // hop), :]
                dst_ref[r, pl.ds(pl.multiple_of(c * QBLK, QBLK), QBLK), :] = val.astype(dst_ref.dtype)
        return carry

    lax.fori_loop(0, per // QBLK, regroup, 0)
    lane = lax.broadcasted_iota(jnp.int32, (QBLK, LANES), 1)
    head0 = lane < HEAD_DIM
    ones_keys = jnp.ones((KWIN, LANES), BF16)
    row = lax.broadcasted_iota(jnp.int32, (QBLK, KWIN), 0)
    col = lax.broadcasted_iota(jnp.int32, (QBLK, KWIN), 1)
    for pat, d in enumerate(DILATIONS):
        nslab = RESIDUES // d
        qp, kp = QBLK // nslab, KWIN // nslab
        qpos = nslab * (row % qp) + row // qp
        kpos = col if d == 1 else nslab * (col % kp) + col // kp
        for which, shift in enumerate((HALF, 0, 2 * HALF)):
            diff = kpos - qpos - shift
            bias_ref[3 * pat + which] = jnp.where((diff >= -HALF) & (diff <= HALF), 0.0, NEG)

    for pat, d in enumerate(DILATIONS):
        nslab = RESIDUES // d
        qp, kp = QBLK // nslab, KWIN // nslab
        nblk = seq // d // QBLK

        def locate(idx, nblk=nblk):
            return idx // nblk, idx % nblk

        def slab_rows(seg, start, size, d=d, nslab=nslab):
            return [(seg + d * j, pl.ds(start, size)) for j in range(nslab)]

        def gather(ref, pieces):
            parts = [ref[slab, rows, :] for slab, rows in pieces]
            return parts[0] if len(parts) == 1 else jnp.concatenate(parts, axis=0)

        def query_pieces(seg, n, qp=qp):
            return slab_rows(seg, pl.multiple_of(n * qp, qp), qp)

        def key_block(ref, ref_n, seg, n, d=d, qp=qp, kp=kp):
            if d == 1:
                start = jnp.clip(n * QBLK - HALF, 0, seq - KWIN)
                return ref_n[pl.ds(pl.multiple_of(start, HALF), KWIN), :]
            start = jnp.clip(n * qp - kp // 4, 0, per - kp)
            return gather(ref, slab_rows(seg, pl.multiple_of(start, kp // 4), kp))

        def stage_scores(g, slot, pat=pat, nblk=nblk):
            for u in range(ATTN_GROUP):
                seg, n = locate(g * ATTN_GROUP + u)
                qb = gather(q_ref, query_pieces(seg, n)).astype(BF16)
                kb = key_block(k_ref, kb_ref, seg, n)
                bias = bias_ref[3 * pat + jnp.where(n == 0, 1, jnp.where(n == nblk - 1, 2, 0))]
                zero = jnp.zeros_like(qb)
                qq = jnp.concatenate([jnp.where(head0, qb, zero), jnp.where(head0, zero, qb)], axis=0)
                s = lax.dot_general(qq, kb, (((1,), (1,)), ((), ())), preferred_element_type=F32)
                s_scr[slot, u, :QBLK, :] = s[:QBLK] + bias
                s_scr[slot, u, QBLK:, :] = s[QBLK:] + bias

        def stage_softmax(slot):
            for u in range(ATTN_GROUP):
                s = s_scr[slot, u]
                m = jnp.max(s, axis=-1, keepdims=True)
                p_scr[slot, u] = jnp.exp(s - m).astype(BF16)
                m_scr[slot, u] = jnp.where(head0, m[:QBLK], m[QBLK:])

        def stage_output(g, slot, d=d, qp=qp, nslab=nslab):
            results = []
            for u in range(ATTN_GROUP):
                seg, n = locate(g * ATTN_GROUP + u)
                v_ones = jnp.concatenate([key_block(v_ref, vb_ref, seg, n), ones_keys], axis=1)
                pv = _dot(p_scr[slot, u], v_ones)
                acc_new = jnp.where(head0, pv[:QBLK, :LANES], pv[QBLK:, :LANES])
                l_new = jnp.where(head0, pv[:QBLK, LANES:], pv[QBLK:, LANES:])
                m_new = m_scr[slot, u]
                pieces = query_pieces(seg, n)
                if d == DILATIONS[0]:
                    results.append((pieces, acc_new, m_new, l_new))
                    continue
                m_old = jnp.concatenate([macc[slab, rows, :] for slab, rows in pieces], axis=0)
                acc_old = jnp.concatenate([oacc[slab, rows, :] for slab, rows in pieces], axis=0)
                l_old = jnp.concatenate([lacc[slab, rows, :] for slab, rows in pieces], axis=0)
                m_tot = jnp.maximum(m_old, m_new)
                w_old = jnp.exp(m_old - m_tot)
                w_new = jnp.exp(m_new - m_tot)
                acc_tot = acc_old * w_old + acc_new * w_new
                l_tot = l_old * w_old + l_new * w_new
                if d == DILATIONS[-1]:
                    results.append((pieces, acc_tot / l_tot, None, None))
                else:
                    results.append((pieces, acc_tot, m_tot, l_tot))
            for (pieces, acc_val, m_val, l_val), u in zip(results, range(ATTN_GROUP)):
                for k, (slab, rows) in enumerate(pieces):
                    part = slice(k * qp, (k + 1) * qp)
                    if d == DILATIONS[-1]:
                        n = (g * ATTN_GROUP + u) % (seq // QBLK)
                        o_ref[0, pl.ds(n * QBLK + k, qp, stride=nslab), :] = acc_val[part]
                    else:
                        oacc[slab, rows, :] = acc_val[part]
                        macc[slab, rows, :] = m_val[part]
                        lacc[slab, rows, :] = l_val[part]

        ngroups = d * nblk // ATTN_GROUP
        stage_scores(0, 0)
        stage_scores(1, 1)
        stage_softmax(0)

        def steady(t, carry):
            g = 2 * t + 2
            stage_scores(g, 0)
            stage_softmax(1)
            stage_output(g - 2, 0)
            stage_scores(g + 1, 1)
            stage_softmax(0)
            stage_output(g - 1, 1)
            return carry

        lax.fori_loop(0, (ngroups - 2) // 2, steady, 0, unroll=ATTN_UNROLL)
        stage_softmax(1)
        stage_output(ngroups - 2, 0)
        stage_output(ngroups - 1, 1)


def _attn_call(q, k, v):
    b, s, d = q.shape
    res, per = RESIDUES, s // RESIDUES
    spec = pl.BlockSpec((1, s, LANES), lambda bi, hp: (bi, 0, hp))
    return pl.pallas_call(
        functools.partial(_attn_kernel, seq=s),
        out_shape=jax.ShapeDtypeStruct(q.shape, F32),
        grid=(b, d // LANES),
        in_specs=[spec, spec, spec],
        out_specs=pl.BlockSpec((1, s, LANES), lambda bi, hp: (bi, 0, hp), pipeline_mode=pl.Buffered(1)),
        scratch_shapes=[
            pltpu.VMEM((res, per, LANES), F32),
            pltpu.VMEM((res, per, LANES), BF16),
            pltpu.VMEM((res, per, LANES), BF16),
            pltpu.VMEM((res, per, LANES), F32),
            pltpu.VMEM((res, per, LANES), F32),
            pltpu.VMEM((res, per, LANES), F32),
            pltpu.VMEM((3 * len(DILATIONS), QBLK, KWIN), F32),
            pltpu.VMEM((2, ATTN_GROUP, 2 * QBLK, KWIN), F32),
            pltpu.VMEM((2, ATTN_GROUP, 2 * QBLK, KWIN), BF16),
            pltpu.VMEM((2, ATTN_GROUP, QBLK, LANES), F32),
            pltpu.VMEM((QBLK * RESIDUES, LANES), F32),
            pltpu.VMEM((s, LANES), BF16),
            pltpu.VMEM((s, LANES), BF16),
        ],
        compiler_params=_params(("arbitrary", "arbitrary"), vmem=ATTN_VMEM_LIMIT),
        name="dilated_attn",
    )(q, k, v)


def _rope_tables(seq):
    pos = jnp.arange(seq, dtype=F32)
    inv_freq = ROPE_THETA ** (-jnp.arange(0, ROT_DIM, 2, dtype=F32) / ROT_DIM)
    ang = pos[:, None] * inv_freq[None, :]
    cos, sin = jnp.cos(ang), jnp.sin(ang)
    half = ROT_DIM // 2
    l64 = jnp.arange(LANES) % HEAD_DIM
    f = l64 % half
    cos_t = jnp.where(l64 < ROT_DIM, cos[:, f], 1.0)
    sin_t = jnp.where(l64 < half, -sin[:, f], jnp.where(l64 < ROT_DIM, sin[:, f], 0.0))
    return cos_t, sin_t


def kernel(x, mix_norm_g, mlp_norm_g, mlp_w1, mlp_w2, ab_w_in, a_spatial_w, a_spatial_b, a_vnorm_g, a_vnorm_b, b_conv_w, b_conv_b, b_norm_g, b_norm_b, ab_w_out, c_w_qkv, c_q_norm_g, c_k_norm_g, c_w_out):
    b, s, d = x.shape
    depth = mix_norm_g.shape[0]
    cos_t, sin_t = _rope_tables(s)
    w1s, w2s = mlp_w1.astype(BF16), mlp_w2.astype(BF16)
    wins, wouts = ab_w_in.astype(BF16), ab_w_out.astype(BF16)
    wqkvs, wos = c_w_qkv.astype(BF16), c_w_out.astype(BF16)
    for layer in range(depth):
        i = layer // 2
        g_mix = mix_norm_g[layer][None, :]
        g_mlp = mlp_norm_g[layer][None, :]
        if layer % 2 == 0:
            spw = a_spatial_w[i].astype(BF16).reshape(A_GROUPS // 2, 2 * CHUNK, CHUNK)
            spb = jnp.repeat(a_spatial_b[i].T, A_WIDTH // A_GROUPS, axis=1)
            x = _mixer_mlp_call(
                x.reshape(b * s, d), s, g_mix, wins, i, spw, spb,
                a_vnorm_g[i][None, :], a_vnorm_b[i][None, :], b_conv_w[i], b_conv_b[i][None, :],
                b_norm_g[i][None, :], b_norm_b[i][None, :], wouts, g_mlp, w1s, w2s, layer).reshape(b, s, d)
        else:
            qg = jnp.tile(c_q_norm_g[i], LANES // HEAD_DIM)[None, :]
            kg = jnp.tile(c_k_norm_g[i], LANES // HEAD_DIM)[None, :]
            q, k, v = _qkv_call(x.reshape(b * s, d), g_mix, wqkvs, i, qg, kg, cos_t, sin_t, s)
            a = _attn_call(q.reshape(b, s, d), k.reshape(b, s, d), v.reshape(b, s, d))
            x = _proj_mlp_call(x.reshape(b * s, d), a.reshape(b * s, d), wos, i, g_mlp, w1s, w2s,
                               layer).reshape(b, s, d)
    return x
```
